```python
import jax, jax.numpy as jnp
from jax import lax
import numpy as np

D_MODEL = 1024
BATCH = 32
SEQ = 256
DEPTH = 1
DEC_BATCH = 8
DEC_SEQ = 4096
PAST_LEN = 512

GRID_W = 64
M_HEADS = 8
M_DK = 64
M_DV = 128
M_WIDTH = M_HEADS * M_DV
CHUNK = 64
A_HEADS = 16
KV_HEADS = 4
GROUP = A_HEADS // KV_HEADS
HEAD_DIM = 64
A_WIDTH = A_HEADS * HEAD_DIM
ROPE_THETA = 10000.0
Q_BLOCK = 128
D_FF = 2816
N_MOD = 9
EPS = 1e-6
MIX_SIZES = (M_HEADS * M_DK, M_HEADS * M_DK, M_WIDTH, M_WIDTH, 4 * M_HEADS,
             A_WIDTH, KV_HEADS * HEAD_DIM, KV_HEADS * HEAD_DIM, 2 * D_MODEL)
D_PROJ = sum(MIX_SIZES)

kernel_name = "hybrid_mlstm_gqa_flow_step"


def rms_norm(x, gain):
    xf = x.astype(jnp.float32)
    y = xf * lax.rsqrt(jnp.mean(xf * xf, axis=-1, keepdims=True) + EPS)
    return (y * gain.astype(jnp.float32)).astype(x.dtype)


def swiglu(h, w_in, w_out):
    a, b = jnp.split(h @ w_in, 2, axis=-1)
    return (jax.nn.silu(a) * b) @ w_out


def axial_rope(x):
    B, T, H, hd = x.shape
    rows = T // GRID_W
    r, c = jnp.meshgrid(jnp.arange(rows), jnp.arange(GRID_W), indexing="ij")
    pos = jnp.stack([r.reshape(-1), c.reshape(-1)], axis=-1).astype(jnp.float32)
    half = hd // 2
    pairs = half // 2
    inv_freq = ROPE_THETA ** (-jnp.arange(pairs, dtype=jnp.float32) / pairs)
    ang = pos[:, :, None] * inv_freq
    cos = jnp.cos(ang)[None, :, None]
    sin = jnp.sin(ang)[None, :, None]
    xf = x.astype(jnp.float32).reshape(B, T, H, 2, half)
    x1, x2 = xf[..., :pairs], xf[..., pairs:]
    out = jnp.concatenate([x1 * cos - x2 * sin, x2 * cos + x1 * sin], axis=-1)
    return out.reshape(B, T, H, hd).astype(x.dtype)


def block_attention(q, k, v):
    B, T, H, hd = q.shape
    nb = T // Q_BLOCK
    qb = jnp.moveaxis(q.reshape(B, nb, Q_BLOCK, KV_HEADS, GROUP, hd), 1, 0)
    scale = hd ** -0.5

    def one_block(qi):
        s = jnp.einsum("bqkgd,bskd->bkgqs", qi, k).astype(jnp.float32) * scale
        p = jax.nn.softmax(s, axis=-1).astype(v.dtype)
        return jnp.einsum("bkgqs,bskd->bqkgd", p, v)

    o = lax.map(one_block, qb)
    return jnp.moveaxis(o, 0, 1).reshape(B, T, H, hd)


def mlstm_scan(q, k, v, log_f, log_i, C0, n0, m0):
    B, H, T, DK = q.shape
    DV = v.shape[-1]
    nc = T // CHUNK

    def to_chunks(a):
        return jnp.moveaxis(a.reshape(B, H, nc, CHUNK, *a.shape[3:]), 2, 0)

    xs = tuple(to_chunks(a) for a in (q, k, v, log_f, log_i))
    lower = jnp.tril(jnp.ones((CHUNK, CHUNK), dtype=bool))

    def step(carry, inp):
        C, n, m = carry
        qc, kc, vc, fc, ic = inp
        b = jnp.cumsum(fc, axis=-1)
        dmat = jnp.where(lower, b[..., :, None] - b[..., None, :] + ic[..., None, :], -jnp.inf)
        inter = b + m[..., None]
        m_t = jnp.maximum(inter, jnp.max(dmat, axis=-1))
        w_inter = jnp.exp(inter - m_t)
        s = jnp.einsum("bhtd,bhsd->bhts", qc, kc) * jnp.exp(dmat - m_t[..., None])
        num = (w_inter[..., None] * jnp.einsum("bhtd,bhde->bhte", qc, C)
               + jnp.einsum("bhts,bhse->bhte", s, vc))
        den = w_inter * jnp.einsum("bhtd,bhd->bht", qc, n) + jnp.sum(s, axis=-1)
        h = num / jnp.maximum(jnp.abs(den), jnp.exp(-m_t))[..., None]
        b_last = b[..., -1]
        g = b_last[..., None] - b + ic
        m_new = jnp.maximum(b_last + m, jnp.max(g, axis=-1))
        a_prev = jnp.exp(b_last + m - m_new)
        kw = kc * jnp.exp(g - m_new[..., None])[..., None]
        C_new = a_prev[..., None, None] * C + jnp.einsum("bhsd,bhse->bhde", kw, vc)
        n_new = a_prev[..., None] * n + jnp.sum(kw, axis=2)
        return (C_new, n_new, m_new), h

    (C, n, m), hs = lax.scan(step, (C0, n0, m0), xs)
    h = jnp.moveaxis(hs, 0, 2).reshape(B, H, T, DV)
    return h, C, n, m


def mlstm_bidir(q, k, v, gates, C0, n0, m0):
    hs, Cs, ns, ms = [], [], [], []
    for d in range(2):
        log_i = jnp.moveaxis(gates[:, :, d, 0], 1, 2)
        log_f = jax.nn.log_sigmoid(jnp.moveaxis(gates[:, :, d, 1], 1, 2))
        seq = (q, k, v, log_f, log_i)
        if d == 1:
            seq = tuple(jnp.flip(a, axis=2) for a in seq)
        h, C, n, m = mlstm_scan(*seq, C0[:, d], n0[:, d], m0[:, d])
        if d == 1:
            h = jnp.flip(h, axis=2)
        hs.append(h); Cs.append(C); ns.append(n); ms.append(m)
    return hs[0] + hs[1], jnp.stack(Cs, 1), jnp.stack(ns, 1), jnp.stack(ms, 1)


def token_mixer(h, w_in, gate_bias, m_norm, qk_norm, w_bm, w_ba, w_out, ctx):
    B, T, _ = h.shape
    f32 = jnp.float32
    idx = np.cumsum(MIX_SIZES)[:-1].tolist()
    mq, mk, mv, mo, mg, aq, ak, av, bg = jnp.split(h @ w_in, idx, axis=-1)
    mq = jnp.transpose(mq.reshape(B, T, M_HEADS, M_DK), (0, 2, 1, 3)).astype(f32) * (M_DK ** -0.5)
    mk = jnp.transpose(mk.reshape(B, T, M_HEADS, M_DK), (0, 2, 1, 3)).astype(f32)
    mv = jnp.transpose(mv.reshape(B, T, M_HEADS, M_DV), (0, 2, 1, 3)).astype(f32)
    mg = mg.reshape(B, T, 2, 2, M_HEADS).astype(f32) + gate_bias.astype(f32)
    if ctx is None:
        C0 = jnp.zeros((B, 2, M_HEADS, M_DK, M_DV), f32)
        n0 = jnp.zeros((B, 2, M_HEADS, M_DK), f32)
        m0 = jnp.zeros((B, 2, M_HEADS), f32)
    else:
        C0, n0, m0 = (a.astype(f32) for a in ctx[2:])
    hm, C, n, m = mlstm_bidir(mq, mk, mv, mg, C0, n0, m0)
    hm = rms_norm(jnp.transpose(hm, (0, 2, 1, 3)), m_norm.reshape(M_HEADS, M_DV))
    hm = (hm * jax.nn.sigmoid(mo.reshape(B, T, M_HEADS, M_DV).astype(f32))).reshape(B, T, M_WIDTH)
    aq = rms_norm(aq.reshape(B, T, A_HEADS, HEAD_DIM), qk_norm[0])
    ak = rms_norm(ak.reshape(B, T, KV_HEADS, HEAD_DIM), qk_norm[1])
    av = av.reshape(B, T, KV_HEADS, HEAD_DIM)
    if ctx is None:
        k_all, v_all = ak, av
    else:
        aq = axial_rope(aq)
        k_all = jnp.concatenate([axial_rope(ak), ctx[0].astype(ak.dtype)], axis=1)
        v_all = jnp.concatenate([av, ctx[1].astype(av.dtype)], axis=1)
    ha = block_attention(aq, k_all, v_all).reshape(B, T, A_WIDTH)
    g = jax.nn.sigmoid(bg.reshape(B, T, 2, D_MODEL))
    y = g[:, :, 0] * (hm.astype(h.dtype) @ w_bm) + g[:, :, 1] * (ha @ w_ba)
    return y @ w_out, (ak, av, C, n, m)


def trunk_layer(x, mod, n_pre, n_post, f_in, f_out, mix_w, ctx):
    sh0, sc0, g0, sh1, sc1, g1, sh2, sc2, g2 = jnp.split(mod, N_MOD, axis=-1)
    h = rms_norm(x, n_pre[0]) * (1 + sc0) + sh0
    x = x + 0.5 * g0 * rms_norm(swiglu(h, f_in[0], f_out[0]), n_post[0])
    h = rms_norm(x, n_pre[1]) * (1 + sc1) + sh1
    y, st = token_mixer(h, *mix_w, ctx)
    x = x + g1 * rms_norm(y, n_post[1])
    h = rms_norm(x, n_pre[2]) * (1 + sc2) + sh2
    x = x + 0.5 * g2 * rms_norm(swiglu(h, f_in[1], f_out[1]), n_post[2])
    return x, st


def setup_inputs(seed: int = 0) -> dict:
    key = jax.random.key(seed)
    ks = jax.random.split(key, 24)
    nrm = jax.random.normal
    f_bias_base = jnp.stack([jnp.zeros((M_HEADS,)), jnp.linspace(3.0, 6.0, M_HEADS)])
    return {
        "x_prompt": nrm(ks[0], (BATCH, SEQ, D_MODEL), jnp.float32),
        "x_sample": nrm(ks[1], (DEC_BATCH, DEC_SEQ, D_MODEL), jnp.float32),
        "c": nrm(ks[2], (DEC_BATCH, D_MODEL), jnp.float32),
        "cache_k": nrm(ks[3], (DEC_BATCH, DEPTH, PAST_LEN, KV_HEADS, HEAD_DIM), jnp.float32),
        "cache_v": nrm(ks[4], (DEC_BATCH, DEPTH, PAST_LEN, KV_HEADS, HEAD_DIM), jnp.float32),
        "state_C": 0.5 * nrm(ks[5], (DEC_BATCH, DEPTH, 2, M_HEADS, M_DK, M_DV), jnp.float32),
        "state_n": 0.5 * nrm(ks[6], (DEC_BATCH, DEPTH, 2, M_HEADS, M_DK), jnp.float32),
        "state_m": 0.5 * nrm(ks[7], (DEC_BATCH, DEPTH, 2, M_HEADS), jnp.float32),
        "c_ctx": nrm(ks[8], (D_MODEL,), jnp.float32),
        "w_mod": 0.5 * D_MODEL ** -0.5 * nrm(ks[9], (DEPTH, D_MODEL, N_MOD * D_MODEL), jnp.float32),
        "b_mod": 0.02 * nrm(ks[10], (DEPTH, N_MOD * D_MODEL), jnp.float32),
        "norm_pre": 1.0 + 0.05 * nrm(ks[11], (DEPTH, 3, D_MODEL), jnp.float32),
        "norm_post": 1.0 + 0.05 * nrm(ks[12], (DEPTH, 3, D_MODEL), jnp.float32),
        "ffn_w_in": D_MODEL ** -0.5 * nrm(ks[13], (DEPTH, 2, D_MODEL, 2 * D_FF), jnp.float32),
        "ffn_w_out": D_FF ** -0.5 * nrm(ks[14], (DEPTH, 2, D_FF, D_MODEL), jnp.float32),
        "w_in_mix": D_MODEL ** -0.5 * nrm(ks[15], (DEPTH, D_MODEL, D_PROJ), jnp.float32),
        "mlstm_gate_bias": f_bias_base + 0.1 * nrm(ks[16], (DEPTH, 2, 2, M_HEADS), jnp.float32),
        "mlstm_norm": 1.0 + 0.05 * nrm(ks[17], (DEPTH, M_WIDTH), jnp.float32),
        "qk_norm": 1.0 + 0.05 * nrm(ks[18], (DEPTH, 2, HEAD_DIM), jnp.float32),
        "w_branch_m": M_WIDTH ** -0.5 * nrm(ks[19], (DEPTH, M_WIDTH, D_MODEL), jnp.float32),
        "w_branch_a": A_WIDTH ** -0.5 * nrm(ks[20], (DEPTH, A_WIDTH, D_MODEL), jnp.float32),
        "w_out": D_MODEL ** -0.5 * nrm(ks[21], (DEPTH, D_MODEL, D_MODEL), jnp.float32),
    }


def reference(x_prompt, x_sample, c, cache_k, cache_v, state_C, state_n, state_m, c_ctx,
              w_mod, b_mod, norm_pre, norm_post, ffn_w_in, ffn_w_out, w_in_mix,
              mlstm_gate_bias, mlstm_norm, qk_norm, w_branch_m, w_branch_a, w_out):
    y_prompt = x_prompt
    y_sample = x_sample
    ks_, vs_, Cs_, ns_, ms_ = [], [], [], [], []
    for l in range(DEPTH):
        mix_w = (w_in_mix[l], mlstm_gate_bias[l], mlstm_norm[l], qk_norm[l],
                 w_branch_m[l], w_branch_a[l], w_out[l])
        mod_ctx = (jax.nn.silu(c_ctx) @ w_mod[l] + b_mod[l])[None, None, :]
        mod_lat = (jax.nn.silu(c) @ w_mod[l] + b_mod[l])[:, None, :]
        y_prompt, (k_new, v_new, C_new, n_new, m_new) = trunk_layer(
            y_prompt, mod_ctx, norm_pre[l], norm_post[l], ffn_w_in[l], ffn_w_out[l], mix_w, None)
        ks_.append(k_new); vs_.append(v_new)
        Cs_.append(C_new); ns_.append(n_new); ms_.append(m_new)
        ctx = (cache_k[:, l], cache_v[:, l], state_C[:, l], state_n[:, l], state_m[:, l])
        y_sample, _ = trunk_layer(
            y_sample, mod_lat, norm_pre[l], norm_post[l], ffn_w_in[l], ffn_w_out[l], mix_w, ctx)
    dt = x_prompt.dtype
    new_cache_k = jnp.stack(ks_, axis=1)
    new_cache_v = jnp.stack(vs_, axis=1)
    new_state_C = jnp.stack(Cs_, axis=1).astype(dt)
    new_state_n = jnp.stack(ns_, axis=1).astype(dt)
    new_state_m = jnp.stack(ms_, axis=1).astype(dt)
    return (y_prompt, y_sample, new_cache_k, new_cache_v, new_state_C, new_state_n, new_state_m)
```

```python
import functools

import jax
import jax.numpy as jnp
from jax import lax
from jax.experimental import pallas as pl
from jax.experimental.pallas import tpu as pltpu

F32 = jnp.float32
BF16 = jnp.bfloat16

D_MODEL = 1024
D_FF = 2816
M_HEADS = 8
M_DK = 64
M_DV = 128
A_HEADS = 16
KV_HEADS = 4
GROUP = A_HEADS // KV_HEADS
HEAD_DIM = 64
GRID_W = 64
ROPE_THETA = 10000.0
N_MOD = 9
EPS = 1e-6
MIX_SIZES = (512, 512, 1024, 1024, 32, 1024, 256, 256, 2048)

MCHUNK = 128
VMEM_LIMIT_BYTES = 56 * 1024 * 1024

_NT_DIMS = (((1,), (1,)), ((), ()))


def _dot(a, b):
    return jnp.dot(a, b, preferred_element_type=F32)


def _dot_nt(a, b):
    return lax.dot_general(a, b, _NT_DIMS, preferred_element_type=F32)


def _rms_rows(x, gain_row):
    ms = jnp.mean(x * x, axis=-1, keepdims=True)
    return x * lax.rsqrt(ms + EPS) * gain_row


def _sigmoid(x):
    return jax.nn.sigmoid(x)


def _params(*sem):
    return pltpu.CompilerParams(dimension_semantics=sem, vmem_limit_bytes=VMEM_LIMIT_BYTES)


def _resident(shape):
    nd = len(shape)
    return pl.BlockSpec(shape, lambda *_: (0,) * nd, pipeline_mode=pl.Buffered(1))


def _mod_kernel(c_ref, w_ref, b_ref, o_ref):
    c = c_ref[...]
    a = c * _sigmoid(c)
    o_ref[...] = jnp.dot(a, w_ref[...], preferred_element_type=F32,
                         precision=lax.Precision.HIGHEST) + b_ref[...]


def _modulation(cvec, w_mod, b_mod):
    rows = cvec.shape[0]
    n = w_mod.shape[1]
    bn = D_MODEL
    return pl.pallas_call(
        _mod_kernel,
        grid=(n // bn,),
        in_specs=[pl.BlockSpec((rows, D_MODEL), lambda j: (0, 0)),
                  pl.BlockSpec((D_MODEL, bn), lambda j: (0, j)),
                  pl.BlockSpec((1, bn), lambda j: (0, j))],
        out_specs=pl.BlockSpec((rows, bn), lambda j: (0, j)),
        out_shape=jax.ShapeDtypeStruct((rows, n), F32),
        compiler_params=_params("arbitrary"),
        name="modulation",
    )(cvec, w_mod, b_mod.reshape(1, n))


def _ffn_kernel(x_ref, mod_ref, npre_ref, npost_ref, wi_ref, wo_ref, o_ref, *, j0, nchunk):
    x = x_ref[0]
    sh = mod_ref[0, j0:j0 + 1, :]
    sc = mod_ref[0, j0 + 1:j0 + 2, :]
    g = mod_ref[0, j0 + 2:j0 + 3, :]
    h = (_rms_rows(x, npre_ref[...]) * (1.0 + sc) + sh).astype(BF16)
    fc = D_FF // nchunk
    acc = None
    for c in range(nchunk):
        a = _dot(h, wi_ref[:, c * fc:(c + 1) * fc])
        b = _dot(h, wi_ref[:, D_FF + c * fc:D_FF + (c + 1) * fc])
        act = (a * _sigmoid(a) * b).astype(BF16)
        part = _dot(act, wo_ref[c * fc:(c + 1) * fc, :])
        acc = part if acc is None else acc + part
    o_ref[0] = x + 0.5 * g * _rms_rows(acc, npost_ref[...])


def _ffn(x, mod, npre, npost, wi, wo, *, j0, tm, per_batch_mod):
    B, T, _ = x.shape
    mod_idx = (lambda b, i: (b, 0, 0)) if per_batch_mod else (lambda b, i: (0, 0, 0))
    return pl.pallas_call(
        functools.partial(_ffn_kernel, j0=j0, nchunk=2),
        grid=(B, T // tm),
        in_specs=[pl.BlockSpec((1, tm, D_MODEL), lambda b, i: (b, i, 0)),
                  pl.BlockSpec((1, N_MOD, D_MODEL), mod_idx),
                  _resident((1, D_MODEL)), _resident((1, D_MODEL)),
                  _resident(wi.shape), _resident(wo.shape)],
        out_specs=pl.BlockSpec((1, tm, D_MODEL), lambda b, i: (b, i, 0)),
        out_shape=jax.ShapeDtypeStruct(x.shape, F32),
        compiler_params=_params("parallel", "parallel"),
        name="ffn_half_step",
    )(x, mod, npre.reshape(1, D_MODEL), npost.reshape(1, D_MODEL), wi, wo)


_NAT_MQ, _NAT_MV, _NAT_MO, _NAT_BG, _NAT_END = 0, 512, 1536, 2560, 4608
_TR_MK, _TR_AQ, _TR_AK, _TR_AV, _TR_MG, _TR_END = 0, 512, 1536, 1792, 2048, 2080


def _rope_swap(x):
    return jnp.concatenate([x[16:32], x[0:16], x[48:64], x[32:48]], axis=0)


def _proj_kernel(*refs, rope):
    if rope:
        (x_ref, mod_ref, npre_ref, wnat_ref, wtr_ref, gq_ref, gk_ref, cos_ref, sin_ref,
         mq_ref, mv_ref, mo_ref, bg_ref, mkT_ref, aqT_ref, akT_ref, avT_ref, mgT_ref) = refs
    else:
        (x_ref, mod_ref, npre_ref, wnat_ref, wtr_ref, gq_ref, gk_ref,
         mq_ref, mv_ref, mo_ref, bg_ref, mkT_ref, aqT_ref, akT_ref, avT_ref, mgT_ref) = refs
    x = x_ref[0]
    sh = mod_ref[0, 3:4, :]
    sc = mod_ref[0, 4:5, :]
    h = (_rms_rows(x, npre_ref[...]) * (1.0 + sc) + sh).astype(BF16)

    mq_ref[0] = (_dot(h, wnat_ref[:, _NAT_MQ:_NAT_MV]) * (M_DK ** -0.5)).astype(BF16)
    mv_ref[0] = _dot(h, wnat_ref[:, _NAT_MV:_NAT_MO]).astype(BF16)
    mo_ref[0] = _dot(h, wnat_ref[:, _NAT_MO:_NAT_BG]).astype(BF16)
    half = (_NAT_END - _NAT_BG) // 2
    bg_ref[0, :, 0:half] = _dot(h, wnat_ref[:, _NAT_BG:_NAT_BG + half]).astype(BF16)
    bg_ref[0, :, half:2 * half] = _dot(h, wnat_ref[:, _NAT_BG + half:_NAT_END]).astype(BF16)

    mkT_ref[0] = _dot_nt(wtr_ref[_TR_MK:_TR_AQ, :], h).astype(BF16)
    avT_ref[0] = _dot_nt(wtr_ref[_TR_AV:_TR_MG, :], h)
    mgT_ref[0] = _dot_nt(wtr_ref[_TR_MG:_TR_END, :], h)

    def head_norm(xh, gain_col):
        ms = jnp.mean(xh * xh, axis=0, keepdims=True)
        xh = xh * lax.rsqrt(ms + EPS) * gain_col
        if rope:
            xh = xh * cos_ref[...] + _rope_swap(xh) * sin_ref[...]
        return xh

    aqT = _dot_nt(wtr_ref[_TR_AQ:_TR_AK, :], h)
    for hh in range(A_HEADS):
        r0 = hh * HEAD_DIM
        xh = head_norm(aqT[r0:r0 + HEAD_DIM], gq_ref[...])
        aqT_ref[0, r0:r0 + HEAD_DIM, :] = (xh * (HEAD_DIM ** -0.5)).astype(BF16)
    akT = _dot_nt(wtr_ref[_TR_AK:_TR_AV, :], h)
    for hh in range(KV_HEADS):
        r0 = hh * HEAD_DIM
        akT_ref[0, r0:r0 + HEAD_DIM, :] = head_norm(akT[r0:r0 + HEAD_DIM], gk_ref[...])


def _projection(x, mod, npre, wnat, wtr, gq, gk, rope_tables, *, tm, per_batch_mod):
    B, T, _ = x.shape
    rope = rope_tables is not None
    mod_idx = (lambda b, i: (b, 0, 0)) if per_batch_mod else (lambda b, i: (0, 0, 0))
    tok = lambda n: pl.BlockSpec((1, tm, n), lambda b, i: (b, i, 0))
    feat = lambda n: pl.BlockSpec((1, n, tm), lambda b, i: (b, 0, i))
    in_specs = [tok(D_MODEL), pl.BlockSpec((1, N_MOD, D_MODEL), mod_idx),
                _resident((1, D_MODEL)), _resident(wnat.shape), _resident(wtr.shape),
                _resident((HEAD_DIM, 1)), _resident((HEAD_DIM, 1))]
    args = [x, mod, npre.reshape(1, D_MODEL), wnat, wtr, gq, gk]
    if rope:
        in_specs += [pl.BlockSpec((HEAD_DIM, tm), lambda b, i: (0, i))] * 2
        args += list(rope_tables)
    out_shape = [jax.ShapeDtypeStruct((B, T, 512), BF16),
                 jax.ShapeDtypeStruct((B, T, 1024), BF16),
                 jax.ShapeDtypeStruct((B, T, 1024), BF16),
                 jax.ShapeDtypeStruct((B, T, 2048), BF16),
                 jax.ShapeDtypeStruct((B, 512, T), BF16),
                 jax.ShapeDtypeStruct((B, 1024, T), BF16),
                 jax.ShapeDtypeStruct((B, 256, T), F32),
                 jax.ShapeDtypeStruct((B, 256, T), F32),
                 jax.ShapeDtypeStruct((B, 32, T), F32)]
    out_specs = [tok(512), tok(1024), tok(1024), tok(2048),
                 feat(512), feat(1024), feat(256), feat(256), feat(32)]
    return pl.pallas_call(
        functools.partial(_proj_kernel, rope=rope),
        grid=(B, T // tm),
        in_specs=in_specs, out_specs=out_specs, out_shape=out_shape,
        compiler_params=_params("parallel", "parallel"),
        name="mixer_in_proj",
    )(*args)


def _log_sigmoid(x):
    return jnp.minimum(x, 0.0) - jnp.log1p(jnp.exp(-jnp.abs(x)))


def _gate_kernel(g_ref, bias_ref, rows_ref, stats_ref, *, L):
    g = g_ref[0] + bias_ref[...]
    T = g.shape[1]
    lane = lax.broadcasted_iota(jnp.int32, (M_HEADS, T), 1) & (L - 1)
    for d in range(2):
        li = g[16 * d:16 * d + 8]
        lf = _log_sigmoid(g[16 * d + 8:16 * d + 16])

        def scan(v, op, fill):
            k = 1
            while k < L:
                if d == 0:
                    sh, ok = pltpu.roll(v, k, 1), lane >= k
                else:
                    sh, ok = pltpu.roll(v, T - k, 1), lane < L - k
                v = op(v, jnp.where(ok, sh, fill))
                k *= 2
            return v

        cum = scan(lf, jnp.add, 0.0)
        a = li - cum
        cm = scan(a, jnp.maximum, -jnp.inf)
        rows_ref[0, d, 0:8, :] = a
        rows_ref[0, d, 8:16, :] = cm
        rows_ref[0, d, 16:24, :] = cum
        for c in range(T // L):
            e = c * L + (L - 1 if d == 0 else 0)
            stats_ref[0, d, c, 0:8, :] = jnp.broadcast_to(cum[:, e:e + 1], (M_HEADS, 128))
            stats_ref[0, d, c, 8:16, :] = jnp.broadcast_to(cm[:, e:e + 1], (M_HEADS, 128))


def _gate_scan(mgT, bias_col, *, L):
    B, _, T = mgT.shape
    nc = T // L
    return pl.pallas_call(
        functools.partial(_gate_kernel, L=L),
        grid=(B,),
        in_specs=[pl.BlockSpec((1, 32, T), lambda b: (b, 0, 0)), _resident((32, 1))],
        out_specs=[pl.BlockSpec((1, 2, 24, T), lambda b: (b, 0, 0, 0)),
                   pl.BlockSpec((1, 2, nc, 16, 128), lambda b: (b, 0, 0, 0, 0))],
        out_shape=[jax.ShapeDtypeStruct((B, 2, 24, T), F32),
                   jax.ShapeDtypeStruct((B, 2, nc, 16, 128), F32)],
        compiler_params=_params("parallel"),
        name="mlstm_gate_scan",
    )(mgT, bias_col)


def _mlstm_kernel(qf_ref, qb_ref, kf_ref, kb_ref, vf_ref, vb_ref, rf_ref, rb_ref, cf_ref, cb_ref,
                  sf_ref, sb_ref, c0_ref, m0_ref, hf_ref, hb_ref, cst_ref, mst_ref, *, L):
    j = pl.program_id(1)

    @pl.when(j == 0)
    def _():
        cst_ref[...] = c0_ref[...]
        mst_ref[...] = m0_ref[...]

    row_i = lax.broadcasted_iota(jnp.int32, (L, L), 0)
    col_i = lax.broadcasted_iota(jnp.int32, (L, L), 1)
    lane = lax.broadcasted_iota(jnp.int32, (L, 128), 1)
    ones_col = jnp.where(lane == 0, 1.0, 0.0).astype(BF16)
    dirs = ((qf_ref, kf_ref, vf_ref, rf_ref, cf_ref, sf_ref, hf_ref),
            (qb_ref, kb_ref, vb_ref, rb_ref, cb_ref, sb_ref, hb_ref))
    for d, (q_ref, k_ref, v_ref, r_ref, c_ref, s_ref, h_ref) in enumerate(dirs):
        mask = (col_i <= row_i) if d == 0 else (col_i >= row_i)
        tot = s_ref[0, 0, 0, 0:8, :]
        amax = s_ref[0, 0, 0, 8:16, :]
        m_old = mst_ref[0, d]
        m_last = jnp.maximum(m_old, amax)
        a_prev = jnp.exp(m_old - m_last)
        mst_ref[0, d] = tot + m_last
        cols = c_ref[0, 0]
        for h in range(M_HEADS):
            p, e = divmod(h, 2)
            q_pair = q_ref[0, :, 128 * p:128 * p + 128]
            keep = (lane < M_DK) if e == 0 else (lane >= M_DK)
            qm = jnp.where(keep, q_pair, jnp.zeros_like(q_pair))
            k_pair = k_ref[0, 128 * p:128 * p + 128, :]
            k_h = k_ref[0, M_DK * h:M_DK * h + M_DK, :]
            v_aug = jnp.concatenate([v_ref[0, :, M_DV * h:M_DV * h + M_DV], ones_col], axis=1)
            a_row = r_ref[0, 0, h:h + 1, :]
            cm_col = cols[:, 8 + h:9 + h]
            cum_col = cols[:, 16 + h:17 + h]
            m_h = m_old[h:h + 1, :]
            big_m = jnp.maximum(m_h, cm_col)
            decay = jnp.exp(jnp.where(mask, a_row - big_m, -jnp.inf))
            s = _dot(qm, k_pair)
            pmat = (s * decay).astype(BF16)
            w = jnp.exp(m_h - big_m)
            c_pair = cst_ref[0, d, p]
            inter = _dot(qm, c_pair.astype(BF16))
            intra = _dot(pmat, v_aug)
            num_aug = jnp.concatenate([w, w], axis=1) * inter + intra
            den = num_aug[:, M_DV:M_DV + 1]
            floor = jnp.exp(-(cum_col + big_m))
            h_ref[0, :, M_DV * h:M_DV * h + M_DV] = (
                num_aug[:, 0:M_DV] / jnp.maximum(jnp.abs(den), floor))
            k_scaled = (k_h.astype(F32) * jnp.exp(a_row - m_last[h:h + 1, :])).astype(BF16)
            upd = _dot(k_scaled, v_aug)
            ap = a_prev[h:h + 1, :]
            cst_ref[0, d, p, M_DK * e:M_DK * e + M_DK, :] = (
                jnp.concatenate([ap, ap], axis=1) * c_pair[M_DK * e:M_DK * e + M_DK] + upd)


def _mlstm(mq, mkT, mv, rows, cols, stats, c0aug, m0rep, *, L):
    assert L == 128, "decay matrices are built as whole (L, 128) vregs"
    B, T, _ = mq.shape
    nc = T // L
    fwd = lambda b, j: j
    bwd = lambda b, j: nc - 1 - j

    def tok(n, cidx):
        return pl.BlockSpec((1, L, n), lambda b, j: (b, cidx(b, j), 0))

    def feat(n, cidx):
        return pl.BlockSpec((1, n, L), lambda b, j: (b, 0, cidx(b, j)))

    def rows_spec(d, cidx):
        return pl.BlockSpec((1, 1, 24, L), lambda b, j: (b, d, 0, cidx(b, j)))

    def cols_spec(d, cidx):
        return pl.BlockSpec((1, 1, L, 24), lambda b, j: (b, d, cidx(b, j), 0))

    def stats_spec(d, cidx):
        return pl.BlockSpec((1, 1, 1, 16, 128), lambda b, j: (b, d, cidx(b, j), 0, 0))

    state_c = pl.BlockSpec((1, 2, 4, 128, 256), lambda b, j: (b, 0, 0, 0, 0))
    state_m = pl.BlockSpec((1, 2, 8, 128), lambda b, j: (b, 0, 0, 0))
    return pl.pallas_call(
        functools.partial(_mlstm_kernel, L=L),
        grid=(B, nc),
        in_specs=[tok(512, fwd), tok(512, bwd), feat(512, fwd), feat(512, bwd),
                  tok(1024, fwd), tok(1024, bwd),
                  rows_spec(0, fwd), rows_spec(1, bwd), cols_spec(0, fwd), cols_spec(1, bwd),
                  stats_spec(0, fwd), stats_spec(1, bwd), state_c, state_m],
        out_specs=[tok(1024, fwd), tok(1024, bwd), state_c, state_m],
        out_shape=[jax.ShapeDtypeStruct((B, T, 1024), F32),
                   jax.ShapeDtypeStruct((B, T, 1024), F32),
                   jax.ShapeDtypeStruct((B, 2, 4, 128, 256), F32),
                   jax.ShapeDtypeStruct((B, 2, 8, 128), F32)],
        compiler_params=_params("parallel", "arbitrary"),
        name="mlstm_chunks",
    )(mq, mq, mkT, mkT, mv, mv, rows, rows, cols, cols, stats, stats, c0aug, m0rep)


def _attn_kernel(q_ref, k_ref, vT_ref, o_ref, *, kb, tq):
    qT = jnp.concatenate([q_ref[0, g] for g in range(GROUP)], axis=1)
    S = k_ref.shape[2]
    m = l = acc = None
    for j in range(S // kb):
        s = _dot(k_ref[0, 0, j * kb:(j + 1) * kb, :], qT)
        mb = jnp.max(s, axis=0, keepdims=True)
        if j == 0:
            m = mb
            p = jnp.exp(s - m)
            l = jnp.sum(p, axis=0, keepdims=True)
            acc = _dot(vT_ref[0, 0, :, j * kb:(j + 1) * kb], p.astype(BF16))
        else:
            m_new = jnp.maximum(m, mb)
            alpha = jnp.exp(m - m_new)
            p = jnp.exp(s - m_new)
            l = alpha * l + jnp.sum(p, axis=0, keepdims=True)
            acc = alpha * acc + _dot(vT_ref[0, 0, :, j * kb:(j + 1) * kb], p.astype(BF16))
            m = m_new
    oT = acc / l
    o_ref[0] = jnp.concatenate(
        [oT[:, g * tq:(g + 1) * tq].T for g in range(GROUP)], axis=1).astype(BF16)


def _attention(aqT, k, vT, *, tq, kb):
    B, _, T = aqT.shape
    S = k.shape[2]
    q4 = aqT.reshape(B, A_HEADS, HEAD_DIM, T)
    return pl.pallas_call(
        functools.partial(_attn_kernel, kb=kb, tq=tq),
        grid=(B, KV_HEADS, T // tq),
        in_specs=[pl.BlockSpec((1, GROUP, HEAD_DIM, tq), lambda b, kv, i: (b, kv, 0, i)),
                  pl.BlockSpec((1, 1, S, HEAD_DIM), lambda b, kv, i: (b, kv, 0, 0)),
                  pl.BlockSpec((1, 1, HEAD_DIM, S), lambda b, kv, i: (b, kv, 0, 0))],
        out_specs=pl.BlockSpec((1, tq, GROUP * HEAD_DIM), lambda b, kv, i: (b, i, kv)),
        out_shape=jax.ShapeDtypeStruct((B, T, A_HEADS * HEAD_DIM), BF16),
        compiler_params=_params("parallel", "parallel", "arbitrary"),
        name="block_attention",
    )(q4, k, vT)


def _merge_kernel(x_ref, mod_ref, hf_ref, hb_ref, mo_ref, ha_ref, bg_ref, mn_ref, npost_ref,
                  wbm_ref, wba_ref, wout_ref, o_ref):
    hm = hf_ref[0] + hb_ref[0]
    parts = []
    for h in range(M_HEADS):
        xh = hm[:, M_DV * h:M_DV * (h + 1)]
        ms = jnp.mean(xh * xh, axis=-1, keepdims=True)
        parts.append(xh * lax.rsqrt(ms + EPS))
    hn = jnp.concatenate(parts, axis=1) * mn_ref[...]
    hmo = (hn * _sigmoid(mo_ref[0].astype(F32))).astype(BF16)
    ym = _dot(hmo, wbm_ref[...])
    ya = _dot(ha_ref[0], wba_ref[...])
    bg = bg_ref[0].astype(F32)
    y = _sigmoid(bg[:, 0:D_MODEL]) * ym + _sigmoid(bg[:, D_MODEL:2 * D_MODEL]) * ya
    yo = _dot(y.astype(BF16), wout_ref[...])
    g1 = mod_ref[0, 5:6, :]
    o_ref[0] = x_ref[0] + g1 * _rms_rows(yo, npost_ref[...])


def _merge(x, mod, hf, hb, mo, ha, bg, mnorm, npost, wbm, wba, wout, *, tm, per_batch_mod):
    B, T, _ = x.shape
    mod_idx = (lambda b, i: (b, 0, 0)) if per_batch_mod else (lambda b, i: (0, 0, 0))
    tok = lambda n: pl.BlockSpec((1, tm, n), lambda b, i: (b, i, 0))
    return pl.pallas_call(
        _merge_kernel,
        grid=(B, T // tm),
        in_specs=[tok(D_MODEL), pl.BlockSpec((1, N_MOD, D_MODEL), mod_idx),
                  tok(1024), tok(1024), tok(1024), tok(1024), tok(2048),
                  _resident((1, 1024)), _resident((1, D_MODEL)),
                  _resident(wbm.shape), _resident(wba.shape), _resident(wout.shape)],
        out_specs=tok(D_MODEL),
        out_shape=jax.ShapeDtypeStruct(x.shape, F32),
        compiler_params=_params("parallel", "parallel"),
        name="mixer_merge",
    )(x, mod, hf, hb, mo, ha, bg, mnorm.reshape(1, 1024), npost.reshape(1, D_MODEL), wbm, wba, wout)


def _rope_tables(T):
    pairs = HEAD_DIM // 4
    t = jnp.arange(T)
    pos = jnp.stack([t // GRID_W, t % GRID_W], axis=0).astype(F32)
    inv_freq = ROPE_THETA ** (-jnp.arange(pairs, dtype=F32) / pairs)
    ang = pos[:, None, :] * inv_freq[None, :, None]
    cos, sin = jnp.cos(ang), jnp.sin(ang)
    cos_t = jnp.concatenate([cos[0], cos[0], cos[1], cos[1]], axis=0)
    sin_t = jnp.concatenate([-sin[0], sin[0], -sin[1], sin[1]], axis=0)
    return cos_t, sin_t


def _trunk(x, mod, per_batch_mod, w, ctx, *, tm):
    B, T, _ = x.shape
    L = MCHUNK
    x = _ffn(x, mod, w["npre"][0], w["npost"][0], w["wi"][0], w["wo"][0],
             j0=0, tm=tm, per_batch_mod=per_batch_mod)
    rope_tables = _rope_tables(T) if ctx is not None else None
    mq, mv, mo, bg, mkT, aqT, akT, avT, mgT = _projection(
        x, mod, w["npre"][1], w["wnat"], w["wtr"], w["gq"], w["gk"], rope_tables,
        tm=tm, per_batch_mod=per_batch_mod)

    rows, stats = _gate_scan(mgT, w["gate_bias"], L=L)
    cols = jnp.swapaxes(rows, 2, 3)
    if ctx is None:
        c0aug = jnp.zeros((B, 2, 4, 128, 256), F32)
        m0rep = jnp.zeros((B, 2, M_HEADS, 128), F32)
    else:
        c0, n0, m0 = ctx[2], ctx[3], ctx[4]
        pad = jnp.zeros(c0.shape[:-1] + (M_DV - 1,), F32)
        c0aug = jnp.concatenate([c0, n0[..., None], pad], axis=-1).reshape(B, 2, 4, 128, 256)
        m0rep = jnp.broadcast_to(m0[..., None], (B, 2, M_HEADS, 128))
    hf, hb, c_fin, m_fin = _mlstm(mq, mkT, mv, rows, cols, stats, c0aug, m0rep, L=L)

    k_tok = jnp.swapaxes(akT.reshape(B, KV_HEADS, HEAD_DIM, T), 2, 3).astype(BF16)
    vT = avT.reshape(B, KV_HEADS, HEAD_DIM, T).astype(BF16)
    if ctx is not None:
        k_ctx = jnp.transpose(ctx[0], (0, 2, 1, 3)).astype(BF16)
        v_ctx = jnp.transpose(ctx[1], (0, 2, 3, 1)).astype(BF16)
        k_tok = jnp.concatenate([k_tok, k_ctx], axis=2)
        vT = jnp.concatenate([vT, v_ctx], axis=3)
    S = k_tok.shape[2]
    ha = _attention(aqT, k_tok, vT, tq=128, kb=min(S, 512))

    x = _merge(x, mod, hf, hb, mo, ha, bg, w["mnorm"], w["npost"][1], w["wbm"], w["wba"], w["wout"],
               tm=tm, per_batch_mod=per_batch_mod)
    x = _ffn(x, mod, w["npre"][2], w["npost"][2], w["wi"][1], w["wo"][1],
             j0=6, tm=tm, per_batch_mod=per_batch_mod)
    return x, (akT, avT, c_fin, m_fin)


def _layer_weights(norm_pre, norm_post, ffn_w_in, ffn_w_out, w_in_mix, gate_bias, mlstm_norm,
                   qk_norm, w_branch_m, w_branch_a, w_out):
    o = [0]
    for s in MIX_SIZES:
        o.append(o[-1] + s)
    col = lambda i: w_in_mix[:, o[i]:o[i + 1]]
    wnat = jnp.concatenate([col(0), col(2), col(3), col(8)], axis=1).astype(BF16)
    wtr = jnp.concatenate([col(1), col(5), col(6), col(7), col(4)], axis=1).T.astype(BF16)
    return dict(
        npre=norm_pre, npost=norm_post,
        wi=ffn_w_in.astype(BF16), wo=ffn_w_out.astype(BF16),
        wnat=wnat, wtr=wtr,
        gq=qk_norm[0].reshape(HEAD_DIM, 1), gk=qk_norm[1].reshape(HEAD_DIM, 1),
        gate_bias=gate_bias.reshape(32, 1), mnorm=mlstm_norm,
        wbm=w_branch_m.astype(BF16), wba=w_branch_a.astype(BF16), wout=w_out.astype(BF16))


def kernel(x_prompt, x_sample, c, cache_k, cache_v, state_C, state_n, state_m, c_ctx, w_mod, b_mod,
           norm_pre, norm_post, ffn_w_in, ffn_w_out, w_in_mix, mlstm_gate_bias, mlstm_norm, qk_norm,
           w_branch_m, w_branch_a, w_out):
    depth = w_mod.shape[0]
    nb = c.shape[0]
    y_prompt, y_sample = x_prompt, x_sample
    ks, vs, cs, ns, ms = [], [], [], [], []
    for l in range(depth):
        w = _layer_weights(norm_pre[l], norm_post[l], ffn_w_in[l], ffn_w_out[l], w_in_mix[l],
                           mlstm_gate_bias[l], mlstm_norm[l], qk_norm[l], w_branch_m[l],
                           w_branch_a[l], w_out[l])
        cvec = jnp.concatenate([c_ctx[None, :], c, jnp.zeros((16 - 1 - nb, D_MODEL), F32)], axis=0)
        mod = _modulation(cvec, w_mod[l], b_mod[l]).reshape(16, N_MOD, D_MODEL)
        y_prompt, (akT, avT, c_fin, m_fin) = _trunk(y_prompt, mod[0:1], False, w, None, tm=256)
        bp, tp = x_prompt.shape[0], x_prompt.shape[1]
        ks.append(jnp.swapaxes(akT, 1, 2).reshape(bp, tp, KV_HEADS, HEAD_DIM))
        vs.append(jnp.swapaxes(avT, 1, 2).reshape(bp, tp, KV_HEADS, HEAD_DIM))
        c_fin = c_fin.reshape(bp, 2, M_HEADS, M_DK, 2 * M_DV)
        cs.append(c_fin[..., 0:M_DV])
        ns.append(c_fin[..., M_DV])
        ms.append(m_fin[..., 0])
        ctx = (cache_k[:, l], cache_v[:, l], state_C[:, l], state_n[:, l], state_m[:, l])
        y_sample, _ = _trunk(y_sample, mod[1:1 + nb], True, w, ctx, tm=512)
    dt = x_prompt.dtype
    return (y_prompt, y_sample, jnp.stack(ks, axis=1), jnp.stack(vs, axis=1),
            jnp.stack(cs, axis=1).astype(dt), jnp.stack(ns, axis=1).astype(dt),
            jnp.stack(ms, axis=1).astype(dt))
```

```python
import functools

import jax
import jax.numpy as jnp
from jax import lax
from jax.experimental import pallas as pl
from jax.experimental.pallas import tpu as pltpu

F32 = jnp.float32
BF16 = jnp.bfloat16

D_MODEL = 1024
D_FF = 2816
M_HEADS = 8
M_DK = 64
M_DV = 128
A_HEADS = 16
KV_HEADS = 4
GROUP = A_HEADS // KV_HEADS
HEAD_DIM = 64
GRID_W = 64
ROPE_THETA = 10000.0
N_MOD = 9
EPS = 1e-6
LOG2E = 1.4426950408889634
MIX_SIZES = (512, 512, 1024, 1024, 32, 1024, 256, 256, 2048)

MCHUNK = 128
VMEM_LIMIT_BYTES = 56 * 1024 * 1024

_NT_DIMS = (((1,), (1,)), ((), ()))


def _dot(a, b):
    return jnp.dot(a, b, preferred_element_type=F32)


def _dot_nt(a, b):
    return lax.dot_general(a, b, _NT_DIMS, preferred_element_type=F32)


def _rms_rows(x, gain_row):
    ms = jnp.mean(x * x, axis=-1, keepdims=True)
    return x * lax.rsqrt(ms + EPS) * gain_row


def _sigmoid(x):
    return jax.nn.sigmoid(x)


def _params(*sem):
    return pltpu.CompilerParams(dimension_semantics=sem, vmem_limit_bytes=VMEM_LIMIT_BYTES)


def _resident(shape):
    nd = len(shape)
    return pl.BlockSpec(shape, lambda *_: (0,) * nd, pipeline_mode=pl.Buffered(1))


def _mod_kernel(c_ref, w_ref, b_ref, o_ref):
    c = c_ref[...]
    a = c * _sigmoid(c)
    o_ref[...] = jnp.dot(a, w_ref[...], preferred_element_type=F32,
                         precision=lax.Precision.HIGHEST) + b_ref[...]


def _modulation(cvec, w_mod, b_mod):
    rows = cvec.shape[0]
    n = w_mod.shape[1]
    bn = D_MODEL
    return pl.pallas_call(
        _mod_kernel,
        grid=(n // bn,),
        in_specs=[pl.BlockSpec((rows, D_MODEL), lambda j: (0, 0)),
                  pl.BlockSpec((D_MODEL, bn), lambda j: (0, j)),
                  pl.BlockSpec((1, bn), lambda j: (0, j))],
        out_specs=pl.BlockSpec((rows, bn), lambda j: (0, j)),
        out_shape=jax.ShapeDtypeStruct((rows, n), F32),
        compiler_params=_params("arbitrary"),
        name="modulation",
    )(cvec, w_mod, b_mod.reshape(1, n))


def _ffn_kernel(x_ref, mod_ref, npre_ref, npost_ref, wi_ref, wo_ref, o_ref, *, j0, nchunk):
    x = x_ref[0]
    sh = mod_ref[0, j0:j0 + 1, :]
    sc = mod_ref[0, j0 + 1:j0 + 2, :]
    g = mod_ref[0, j0 + 2:j0 + 3, :]
    h = (_rms_rows(x, npre_ref[...]) * (1.0 + sc) + sh).astype(BF16)
    fc = D_FF // nchunk
    acc = None
    for c in range(nchunk):
        a = _dot(h, wi_ref[:, c * fc:(c + 1) * fc])
        b = _dot(h, wi_ref[:, D_FF + c * fc:D_FF + (c + 1) * fc])
        act = (a * _sigmoid(a) * b).astype(BF16)
        part = _dot(act, wo_ref[c * fc:(c + 1) * fc, :])
        acc = part if acc is None else acc + part
    o_ref[0] = x + 0.5 * g * _rms_rows(acc, npost_ref[...])


def _ffn(x, mod, npre, npost, wi, wo, *, j0, tm, per_batch_mod):
    B, T, _ = x.shape
    mod_idx = (lambda b, i: (b, 0, 0)) if per_batch_mod else (lambda b, i: (0, 0, 0))
    return pl.pallas_call(
        functools.partial(_ffn_kernel, j0=j0, nchunk=2),
        grid=(B, T // tm),
        in_specs=[pl.BlockSpec((1, tm, D_MODEL), lambda b, i: (b, i, 0)),
                  pl.BlockSpec((1, N_MOD, D_MODEL), mod_idx),
                  _resident((1, D_MODEL)), _resident((1, D_MODEL)),
                  _resident(wi.shape), _resident(wo.shape)],
        out_specs=pl.BlockSpec((1, tm, D_MODEL), lambda b, i: (b, i, 0)),
        out_shape=jax.ShapeDtypeStruct(x.shape, F32),
        compiler_params=_params("parallel", "parallel"),
        name="ffn_half_step",
    )(x, mod, npre.reshape(1, D_MODEL), npost.reshape(1, D_MODEL), wi, wo)


_NAT_MQ, _NAT_MV, _NAT_MO, _NAT_BG, _NAT_END = 0, 512, 1536, 2560, 4608
_TR_MK, _TR_AQ, _TR_AK, _TR_AV, _TR_MG, _TR_END = 0, 512, 1536, 1792, 2048, 2080


def _rope_swap(x):
    return jnp.concatenate([x[16:32], x[0:16], x[48:64], x[32:48]], axis=0)


def _proj_kernel(*refs, rope):
    if rope:
        (x_ref, mod_ref, npre_ref, wnat_ref, wtr_ref, gq_ref, gk_ref, cos_ref, sin_ref,
         mq_ref, mv_ref, mo_ref, bg_ref, mkT_ref, aqT_ref, akT_ref, avT_ref, mgT_ref) = refs
    else:
        (x_ref, mod_ref, npre_ref, wnat_ref, wtr_ref, gq_ref, gk_ref,
         mq_ref, mv_ref, mo_ref, bg_ref, mkT_ref, aqT_ref, akT_ref, avT_ref, mgT_ref) = refs
    x = x_ref[0]
    sh = mod_ref[0, 3:4, :]
    sc = mod_ref[0, 4:5, :]
    h = (_rms_rows(x, npre_ref[...]) * (1.0 + sc) + sh).astype(BF16)

    mq_ref[0] = (_dot(h, wnat_ref[:, _NAT_MQ:_NAT_MV]) * (M_DK ** -0.5)).astype(BF16)
    mv_ref[0] = _dot(h, wnat_ref[:, _NAT_MV:_NAT_MO]).astype(BF16)
    mo_ref[0] = _dot(h, wnat_ref[:, _NAT_MO:_NAT_BG]).astype(BF16)
    half = (_NAT_END - _NAT_BG) // 2
    bg_ref[0, :, 0:half] = _dot(h, wnat_ref[:, _NAT_BG:_NAT_BG + half]).astype(BF16)
    bg_ref[0, :, half:2 * half] = _dot(h, wnat_ref[:, _NAT_BG + half:_NAT_END]).astype(BF16)

    mkT_ref[0] = _dot_nt(wtr_ref[_TR_MK:_TR_AQ, :], h).astype(BF16)
    avT_ref[0] = _dot_nt(wtr_ref[_TR_AV:_TR_MG, :], h)
    mgT_ref[0] = _dot_nt(wtr_ref[_TR_MG:_TR_END, :], h)

    def head_norm(xh, gain_col):
        ms = jnp.mean(xh * xh, axis=0, keepdims=True)
        xh = xh * lax.rsqrt(ms + EPS) * gain_col
        if rope:
            xh = xh * cos_ref[...] + _rope_swap(xh) * sin_ref[...]
        return xh

    aqT = _dot_nt(wtr_ref[_TR_AQ:_TR_AK, :], h)
    for hh in range(A_HEADS):
        r0 = hh * HEAD_DIM
        xh = head_norm(aqT[r0:r0 + HEAD_DIM], gq_ref[...])
        aqT_ref[0, r0:r0 + HEAD_DIM, :] = (xh * (LOG2E * HEAD_DIM ** -0.5)).astype(BF16)
    akT = _dot_nt(wtr_ref[_TR_AK:_TR_AV, :], h)
    for hh in range(KV_HEADS):
        r0 = hh * HEAD_DIM
        akT_ref[0, r0:r0 + HEAD_DIM, :] = head_norm(akT[r0:r0 + HEAD_DIM], gk_ref[...])


def _projection(x, mod, npre, wnat, wtr, gq, gk, rope_tables, *, tm, per_batch_mod):
    B, T, _ = x.shape
    rope = rope_tables is not None
    mod_idx = (lambda b, i: (b, 0, 0)) if per_batch_mod else (lambda b, i: (0, 0, 0))
    tok = lambda n: pl.BlockSpec((1, tm, n), lambda b, i: (b, i, 0))
    feat = lambda n: pl.BlockSpec((1, n, tm), lambda b, i: (b, 0, i))
    in_specs = [tok(D_MODEL), pl.BlockSpec((1, N_MOD, D_MODEL), mod_idx),
                _resident((1, D_MODEL)), _resident(wnat.shape), _resident(wtr.shape),
                _resident((HEAD_DIM, 1)), _resident((HEAD_DIM, 1))]
    args = [x, mod, npre.reshape(1, D_MODEL), wnat, wtr, gq, gk]
    if rope:
        in_specs += [pl.BlockSpec((HEAD_DIM, tm), lambda b, i: (0, i))] * 2
        args += list(rope_tables)
    out_shape = [jax.ShapeDtypeStruct((B, T, 512), BF16),
                 jax.ShapeDtypeStruct((B, T, 1024), BF16),
                 jax.ShapeDtypeStruct((B, T, 1024), BF16),
                 jax.ShapeDtypeStruct((B, T, 2048), BF16),
                 jax.ShapeDtypeStruct((B, 512, T), BF16),
                 jax.ShapeDtypeStruct((B, 1024, T), BF16),
                 jax.ShapeDtypeStruct((B, 256, T), F32),
                 jax.ShapeDtypeStruct((B, 256, T), F32),
                 jax.ShapeDtypeStruct((B, 32, T), F32)]
    out_specs = [tok(512), tok(1024), tok(1024), tok(2048),
                 feat(512), feat(1024), feat(256), feat(256), feat(32)]
    return pl.pallas_call(
        functools.partial(_proj_kernel, rope=rope),
        grid=(B, T // tm),
        in_specs=in_specs, out_specs=out_specs, out_shape=out_shape,
        compiler_params=_params("parallel", "parallel"),
        name="mixer_in_proj",
    )(*args)


def _log_sigmoid(x):
    return jnp.minimum(x, 0.0) - jnp.log1p(jnp.exp(-jnp.abs(x)))


def _gate_kernel(g_ref, bias_ref, rows_ref, stats_ref, *, L):
    g = g_ref[0] + bias_ref[...]
    T = g.shape[1]
    lane = lax.broadcasted_iota(jnp.int32, (M_HEADS, T), 1) & (L - 1)
    for d in range(2):
        li = g[16 * d:16 * d + 8]
        lf = _log_sigmoid(g[16 * d + 8:16 * d + 16])

        def scan(v, op, fill):
            k = 1
            while k < L:
                if d == 0:
                    sh, ok = pltpu.roll(v, k, 1), lane >= k
                else:
                    sh, ok = pltpu.roll(v, T - k, 1), lane < L - k
                v = op(v, jnp.where(ok, sh, fill))
                k *= 2
            return v

        cum = scan(lf, jnp.add, 0.0)
        a = li - cum
        cm = scan(a, jnp.maximum, -jnp.inf)
        rows_ref[0, d, 0:8, :] = a
        rows_ref[0, d, 8:16, :] = cm
        rows_ref[0, d, 16:24, :] = cum
        for c in range(T // L):
            e = c * L + (L - 1 if d == 0 else 0)
            stats_ref[0, d, c, 0:8, :] = jnp.broadcast_to(cum[:, e:e + 1], (M_HEADS, 128))
            stats_ref[0, d, c, 8:16, :] = jnp.broadcast_to(cm[:, e:e + 1], (M_HEADS, 128))


def _gate_scan(mgT, bias_col, *, L):
    B, _, T = mgT.shape
    nc = T // L
    return pl.pallas_call(
        functools.partial(_gate_kernel, L=L),
        grid=(B,),
        in_specs=[pl.BlockSpec((1, 32, T), lambda b: (b, 0, 0)), _resident((32, 1))],
        out_specs=[pl.BlockSpec((1, 2, 24, T), lambda b: (b, 0, 0, 0)),
                   pl.BlockSpec((1, 2, nc, 16, 128), lambda b: (b, 0, 0, 0, 0))],
        out_shape=[jax.ShapeDtypeStruct((B, 2, 24, T), F32),
                   jax.ShapeDtypeStruct((B, 2, nc, 16, 128), F32)],
        compiler_params=_params("parallel"),
        name="mlstm_gate_scan",
    )(mgT, bias_col)


def _mlstm_kernel(qf_ref, qb_ref, kf_ref, kb_ref, vf_ref, vb_ref, rf_ref, rb_ref, cf_ref, cb_ref,
                  sf_ref, sb_ref, c0_ref, m0_ref, hf_ref, hb_ref, cst_ref, mst_ref, *, L):
    j = pl.program_id(1)

    @pl.when(j == 0)
    def _():
        cst_ref[...] = c0_ref[...]
        mst_ref[...] = m0_ref[...]

    row_i = lax.broadcasted_iota(jnp.int32, (L, L), 0)
    col_i = lax.broadcasted_iota(jnp.int32, (L, L), 1)
    lane = lax.broadcasted_iota(jnp.int32, (L, 128), 1)
    ones_col = jnp.where(lane == 0, 1.0, 0.0).astype(BF16)
    dirs = ((qf_ref, kf_ref, vf_ref, rf_ref, cf_ref, sf_ref, hf_ref),
            (qb_ref, kb_ref, vb_ref, rb_ref, cb_ref, sb_ref, hb_ref))
    for d, (q_ref, k_ref, v_ref, r_ref, c_ref, s_ref, h_ref) in enumerate(dirs):
        mask = (col_i <= row_i) if d == 0 else (col_i >= row_i)
        tot = s_ref[0, 0, 0, 0:8, :]
        amax = s_ref[0, 0, 0, 8:16, :]
        m_old = mst_ref[0, d]
        m_last = jnp.maximum(m_old, amax)
        a_prev = jnp.exp(m_old - m_last)
        mst_ref[0, d] = tot + m_last
        cols = c_ref[0, 0]
        for h in range(M_HEADS):
            p, e = divmod(h, 2)
            q_pair = q_ref[0, :, 128 * p:128 * p + 128]
            keep = (lane < M_DK) if e == 0 else (lane >= M_DK)
            qm = jnp.where(keep, q_pair, jnp.zeros_like(q_pair))
            k_pair = k_ref[0, 128 * p:128 * p + 128, :]
            k_h = k_ref[0, M_DK * h:M_DK * h + M_DK, :]
            v_aug = jnp.concatenate([v_ref[0, :, M_DV * h:M_DV * h + M_DV], ones_col], axis=1)
            a_row = r_ref[0, 0, h:h + 1, :]
            cm_col = cols[:, 8 + h:9 + h]
            cum_col = cols[:, 16 + h:17 + h]
            m_h = m_old[h:h + 1, :]
            big_m = jnp.maximum(m_h, cm_col)
            decay = jnp.exp(jnp.where(mask, a_row - big_m, -jnp.inf))
            s = _dot(qm, k_pair)
            pmat = (s * decay).astype(BF16)
            w = jnp.exp(m_h - big_m)
            c_pair = cst_ref[0, d, p]
            inter = _dot(qm, c_pair.astype(BF16))
            intra = _dot(pmat, v_aug)
            num_aug = jnp.concatenate([w, w], axis=1) * inter + intra
            den = num_aug[:, M_DV:M_DV + 1]
            floor = jnp.exp(-(cum_col + big_m))
            h_ref[0, :, M_DV * h:M_DV * h + M_DV] = (
                num_aug[:, 0:M_DV] / jnp.maximum(jnp.abs(den), floor))
            k_scaled = (k_h.astype(F32) * jnp.exp(a_row - m_last[h:h + 1, :])).astype(BF16)
            upd = _dot(k_scaled, v_aug)
            ap = a_prev[h:h + 1, :]
            cst_ref[0, d, p, M_DK * e:M_DK * e + M_DK, :] = (
                jnp.concatenate([ap, ap], axis=1) * c_pair[M_DK * e:M_DK * e + M_DK] + upd)


def _mlstm(mq, mkT, mv, rows, cols, stats, c0aug, m0rep, *, L):
    assert L == 128, "decay matrices are built as whole (L, 128) vregs"
    B, T, _ = mq.shape
    nc = T // L
    fwd = lambda b, j: j
    bwd = lambda b, j: nc - 1 - j

    def tok(n, cidx):
        return pl.BlockSpec((1, L, n), lambda b, j: (b, cidx(b, j), 0))

    def feat(n, cidx):
        return pl.BlockSpec((1, n, L), lambda b, j: (b, 0, cidx(b, j)))

    def rows_spec(d, cidx):
        return pl.BlockSpec((1, 1, 24, L), lambda b, j: (b, d, 0, cidx(b, j)))

    def cols_spec(d, cidx):
        return pl.BlockSpec((1, 1, L, 24), lambda b, j: (b, d, cidx(b, j), 0))

    def stats_spec(d, cidx):
        return pl.BlockSpec((1, 1, 1, 16, 128), lambda b, j: (b, d, cidx(b, j), 0, 0))

    state_c = pl.BlockSpec((1, 2, 4, 128, 256), lambda b, j: (b, 0, 0, 0, 0))
    state_m = pl.BlockSpec((1, 2, 8, 128), lambda b, j: (b, 0, 0, 0))
    return pl.pallas_call(
        functools.partial(_mlstm_kernel, L=L),
        grid=(B, nc),
        in_specs=[tok(512, fwd), tok(512, bwd), feat(512, fwd), feat(512, bwd),
                  tok(1024, fwd), tok(1024, bwd),
                  rows_spec(0, fwd), rows_spec(1, bwd), cols_spec(0, fwd), cols_spec(1, bwd),
                  stats_spec(0, fwd), stats_spec(1, bwd), state_c, state_m],
        out_specs=[tok(1024, fwd), tok(1024, bwd), state_c, state_m],
        out_shape=[jax.ShapeDtypeStruct((B, T, 1024), F32),
                   jax.ShapeDtypeStruct((B, T, 1024), F32),
                   jax.ShapeDtypeStruct((B, 2, 4, 128, 256), F32),
                   jax.ShapeDtypeStruct((B, 2, 8, 128), F32)],
        compiler_params=_params("parallel", "arbitrary"),
        name="mlstm_chunks",
    )(mq, mq, mkT, mkT, mv, mv, rows, rows, cols, cols, stats, stats, c0aug, m0rep)


V_ROWS = HEAD_DIM + 16


def _attn_kernel(q_ref, k_ref, vT_ref, o_ref, s0_scr, s1_scr, m0_scr, m1_scr, *, tq, nq, kc, pc):
    i = pl.program_id(2)
    n = GROUP * tq
    S = k_ref.shape[2]
    bufs = ((s0_scr, m0_scr), (s1_scr, m1_scr))

    def run(dst, src):
        if dst is not None:
            qT = jnp.concatenate([q_ref[0, g] for g in range(GROUP)], axis=1)
            pm = None
        if src is not None:
            m = src[1][...]
            acc = None
        for c in range(S // pc):
            if dst is not None:
                for r0 in range(c * pc, (c + 1) * pc, kc):
                    s_c = _dot(k_ref[0, 0, r0:r0 + kc, :], qT)
                    dst[0][r0:r0 + kc, :] = s_c
                    part = jnp.max(s_c.reshape(kc // 8, 8, n), axis=0)
                    pm = part if pm is None else jnp.maximum(pm, part)
            if src is not None:
                p = jnp.exp2(src[0][c * pc:(c + 1) * pc, :] - m).astype(BF16)
                part = _dot(vT_ref[0, 0, :, c * pc:(c + 1) * pc], p)
                acc = part if acc is None else acc + part
        if dst is not None:
            dst[1][...] = jnp.max(pm, axis=0, keepdims=True)
        if src is not None:
            oT = acc[0:HEAD_DIM] / acc[HEAD_DIM:HEAD_DIM + 1]
            o_ref[0] = jnp.concatenate(
                [oT[:, g * tq:(g + 1) * tq].T for g in range(GROUP)], axis=1).astype(BF16)

    @pl.when(i == 0)
    def _():
        run(bufs[0], None)

    for par in range(2):
        @pl.when(jnp.logical_and(jnp.logical_and(i > 0, i < nq), i % 2 == par))
        def _():
            run(bufs[par], bufs[1 - par])

    @pl.when(i == nq)
    def _():
        run(None, bufs[(nq - 1) % 2])


def _attention(aqT, k, vT, *, tq):
    B, _, T = aqT.shape
    S = k.shape[2]
    nq = T // tq
    n = GROUP * tq
    q4 = aqT.reshape(B, A_HEADS, HEAD_DIM, T)
    return pl.pallas_call(
        functools.partial(_attn_kernel, tq=tq, nq=nq, kc=64, pc=256),
        grid=(B, KV_HEADS, nq + 1),
        in_specs=[pl.BlockSpec((1, GROUP, HEAD_DIM, tq),
                               lambda b, kv, i: (b, kv, 0, jnp.minimum(i, nq - 1))),
                  pl.BlockSpec((1, 1, S, HEAD_DIM), lambda b, kv, i: (b, kv, 0, 0)),
                  pl.BlockSpec((1, 1, V_ROWS, S), lambda b, kv, i: (b, kv, 0, 0))],
        out_specs=pl.BlockSpec((1, tq, GROUP * HEAD_DIM),
                               lambda b, kv, i: (b, jnp.maximum(i - 1, 0), kv)),
        out_shape=jax.ShapeDtypeStruct((B, T, A_HEADS * HEAD_DIM), BF16),
        scratch_shapes=[pltpu.VMEM((S, n), F32), pltpu.VMEM((S, n), F32),
                        pltpu.VMEM((1, n), F32), pltpu.VMEM((1, n), F32)],
        compiler_params=_params("parallel", "parallel", "arbitrary"),
        name="block_attention",
    )(q4, k, vT)


def _merge_kernel(x_ref, mod_ref, hf_ref, hb_ref, mo_ref, ha_ref, bg_ref, mn_ref, npost_ref,
                  wbm_ref, wba_ref, wout_ref, o_ref):
    hm = hf_ref[0] + hb_ref[0]
    parts = []
    for h in range(M_HEADS):
        xh = hm[:, M_DV * h:M_DV * (h + 1)]
        ms = jnp.mean(xh * xh, axis=-1, keepdims=True)
        parts.append(xh * lax.rsqrt(ms + EPS))
    hn = jnp.concatenate(parts, axis=1) * mn_ref[...]
    hmo = (hn * _sigmoid(mo_ref[0].astype(F32))).astype(BF16)
    ym = _dot(hmo, wbm_ref[...])
    ya = _dot(ha_ref[0], wba_ref[...])
    bg = bg_ref[0].astype(F32)
    y = _sigmoid(bg[:, 0:D_MODEL]) * ym + _sigmoid(bg[:, D_MODEL:2 * D_MODEL]) * ya
    yo = _dot(y.astype(BF16), wout_ref[...])
    g1 = mod_ref[0, 5:6, :]
    o_ref[0] = x_ref[0] + g1 * _rms_rows(yo, npost_ref[...])


def _merge(x, mod, hf, hb, mo, ha, bg, mnorm, npost, wbm, wba, wout, *, tm, per_batch_mod):
    B, T, _ = x.shape
    mod_idx = (lambda b, i: (b, 0, 0)) if per_batch_mod else (lambda b, i: (0, 0, 0))
    tok = lambda n: pl.BlockSpec((1, tm, n), lambda b, i: (b, i, 0))
    return pl.pallas_call(
        _merge_kernel,
        grid=(B, T // tm),
        in_specs=[tok(D_MODEL), pl.BlockSpec((1, N_MOD, D_MODEL), mod_idx),
                  tok(1024), tok(1024), tok(1024), tok(1024), tok(2048),
                  _resident((1, 1024)), _resident((1, D_MODEL)),
                  _resident(wbm.shape), _resident(wba.shape), _resident(wout.shape)],
        out_specs=tok(D_MODEL),
        out_shape=jax.ShapeDtypeStruct(x.shape, F32),
        compiler_params=_params("parallel", "parallel"),
        name="mixer_merge",
    )(x, mod, hf, hb, mo, ha, bg, mnorm.reshape(1, 1024), npost.reshape(1, D_MODEL), wbm, wba, wout)


def _rope_tables(T):
    pairs = HEAD_DIM // 4
    t = jnp.arange(T)
    pos = jnp.stack([t // GRID_W, t % GRID_W], axis=0).astype(F32)
    inv_freq = ROPE_THETA ** (-jnp.arange(pairs, dtype=F32) / pairs)
    ang = pos[:, None, :] * inv_freq[None, :, None]
    cos, sin = jnp.cos(ang), jnp.sin(ang)
    cos_t = jnp.concatenate([cos[0], cos[0], cos[1], cos[1]], axis=0)
    sin_t = jnp.concatenate([-sin[0], sin[0], -sin[1], sin[1]], axis=0)
    return cos_t, sin_t


def _trunk(x, mod, per_batch_mod, w, ctx, *, tm):
    B, T, _ = x.shape
    L = MCHUNK
    x = _ffn(x, mod, w["npre"][0], w["npost"][0], w["wi"][0], w["wo"][0],
             j0=0, tm=tm, per_batch_mod=per_batch_mod)
    rope_tables = _rope_tables(T) if ctx is not None else None
    mq, mv, mo, bg, mkT, aqT, akT, avT, mgT = _projection(
        x, mod, w["npre"][1], w["wnat"], w["wtr"], w["gq"], w["gk"], rope_tables,
        tm=tm, per_batch_mod=per_batch_mod)

    rows, stats = _gate_scan(mgT, w["gate_bias"], L=L)
    cols = jnp.swapaxes(rows, 2, 3)
    if ctx is None:
        c0aug = jnp.zeros((B, 2, 4, 128, 256), F32)
        m0rep = jnp.zeros((B, 2, M_HEADS, 128), F32)
    else:
        c0, n0, m0 = ctx[2], ctx[3], ctx[4]
        pad = jnp.zeros(c0.shape[:-1] + (M_DV - 1,), F32)
        c0aug = jnp.concatenate([c0, n0[..., None], pad], axis=-1).reshape(B, 2, 4, 128, 256)
        m0rep = jnp.broadcast_to(m0[..., None], (B, 2, M_HEADS, 128))
    hf, hb, c_fin, m_fin = _mlstm(mq, mkT, mv, rows, cols, stats, c0aug, m0rep, L=L)

    k_tok = jnp.swapaxes(akT.reshape(B, KV_HEADS, HEAD_DIM, T), 2, 3).astype(BF16)
    vT = avT.reshape(B, KV_HEADS, HEAD_DIM, T).astype(BF16)
    if ctx is not None:
        k_ctx = jnp.transpose(ctx[0], (0, 2, 1, 3)).astype(BF16)
        v_ctx = jnp.transpose(ctx[1], (0, 2, 3, 1)).astype(BF16)
        k_tok = jnp.concatenate([k_tok, k_ctx], axis=2)
        vT = jnp.concatenate([vT, v_ctx], axis=3)
    S = k_tok.shape[2]
    ones_rows = jnp.concatenate([jnp.ones((B, KV_HEADS, 1, S), BF16),
                                 jnp.zeros((B, KV_HEADS, V_ROWS - HEAD_DIM - 1, S), BF16)], axis=2)
    ha = _attention(aqT, k_tok, jnp.concatenate([vT, ones_rows], axis=2), tq=128)

    x = _merge(x, mod, hf, hb, mo, ha, bg, w["mnorm"], w["npost"][1], w["wbm"], w["wba"], w["wout"],
               tm=tm, per_batch_mod=per_batch_mod)
    x = _ffn(x, mod, w["npre"][2], w["npost"][2], w["wi"][1], w["wo"][1],
             j0=6, tm=tm, per_batch_mod=per_batch_mod)
    return x, (akT, avT, c_fin, m_fin)


def _layer_weights(norm_pre, norm_post, ffn_w_in, ffn_w_out, w_in_mix, gate_bias, mlstm_norm,
                   qk_norm, w_branch_m, w_branch_a, w_out):
    o = [0]
    for s in MIX_SIZES:
        o.append(o[-1] + s)
    col = lambda i: w_in_mix[:, o[i]:o[i + 1]]
    wnat = jnp.concatenate([col(0), col(2), col(3), col(8)], axis=1).astype(BF16)
    wtr = jnp.concatenate([col(1), col(5), col(6), col(7), col(4)], axis=1).T.astype(BF16)
    return dict(
        npre=norm_pre, npost=norm_post,
        wi=ffn_w_in.astype(BF16), wo=ffn_w_out.astype(BF16),
        wnat=wnat, wtr=wtr,
        gq=qk_norm[0].reshape(HEAD_DIM, 1), gk=qk_norm[1].reshape(HEAD_DIM, 1),
        gate_bias=gate_bias.reshape(32, 1), mnorm=mlstm_norm,
        wbm=w_branch_m.astype(BF16), wba=w_branch_a.astype(BF16), wout=w_out.astype(BF16))


def kernel(x_prompt, x_sample, c, cache_k, cache_v, state_C, state_n, state_m, c_ctx, w_mod, b_mod,
           norm_pre, norm_post, ffn_w_in, ffn_w_out, w_in_mix, mlstm_gate_bias, mlstm_norm, qk_norm,
           w_branch_m, w_branch_a, w_out):
    depth = w_mod.shape[0]
    nb = c.shape[0]
    y_prompt, y_sample = x_prompt, x_sample
    ks, vs, cs, ns, ms = [], [], [], [], []
    for l in range(depth):
        w = _layer_weights(norm_pre[l], norm_post[l], ffn_w_in[l], ffn_w_out[l], w_in_mix[l],
                           mlstm_gate_bias[l], mlstm_norm[l], qk_norm[l], w_branch_m[l],
                           w_branch_a[l], w_out[l])
        cvec = jnp.concatenate([c_ctx[None, :], c, jnp.zeros((16 - 1 - nb, D_MODEL), F32)], axis=0)
        mod = _modulation(cvec, w_mod[l], b_mod[l]).reshape(16, N_MOD, D_MODEL)
        y_prompt, (akT, avT, c_fin, m_fin) = _trunk(y_prompt, mod[0:1], False, w, None, tm=256)
        bp, tp = x_prompt.shape[0], x_prompt.shape[1]
        ks.append(jnp.swapaxes(akT, 1, 2).reshape(bp, tp, KV_HEADS, HEAD_DIM))
        vs.append(jnp.swapaxes(avT, 1, 2).reshape(bp, tp, KV_HEADS, HEAD_DIM))
        c_fin = c_fin.reshape(bp, 2, M_HEADS, M_DK, 2 * M_DV)
        cs.append(c_fin[..., 0:M_DV])
        ns.append(c_fin[..., M_DV])
        ms.append(m_fin[..., 0])
        ctx = (cache_k[:, l], cache_v[:, l], state_C[:, l], state_n[:, l], state_m[:, l])
        y_sample, _ = _trunk(y_sample, mod[1:1 + nb], True, w, ctx, tm=512)
    dt = x_prompt.dtype
    return (y_prompt, y_sample, jnp.stack(ks, axis=1), jnp.stack(vs, axis=1),
            jnp.stack(cs, axis=1).astype(dt), jnp.stack(ns, axis=1).astype(dt),
            jnp.stack(ms, axis=1).astype(dt))
```

```python
import functools

import jax
import jax.numpy as jnp
from jax import lax
from jax.experimental import pallas as pl
from jax.experimental.pallas import tpu as pltpu

F32 = jnp.float32
BF16 = jnp.bfloat16

D_MODEL = 1024
D_FF = 2816
M_HEADS = 8
M_DK = 64
M_DV = 128
A_HEADS = 16
KV_HEADS = 4
GROUP = A_HEADS // KV_HEADS
HEAD_DIM = 64
GRID_W = 64
ROPE_THETA = 10000.0
N_MOD = 9
EPS = 1e-6
LOG2E = 1.4426950408889634
MIX_SIZES = (512, 512, 1024, 1024, 32, 1024, 256, 256, 2048)

MCHUNK = 128
VMEM_LIMIT_BYTES = 56 * 1024 * 1024

_NT_DIMS = (((1,), (1,)), ((), ()))


def _dot(a, b):
    return jnp.dot(a, b, preferred_element_type=F32)


def _dot_nt(a, b):
    return lax.dot_general(a, b, _NT_DIMS, preferred_element_type=F32)


def _rms_rows(x, gain_row):
    ms = jnp.mean(x * x, axis=-1, keepdims=True)
    return x * lax.rsqrt(ms + EPS) * gain_row


def _sigmoid(x):
    return jax.nn.sigmoid(x)


def _params(*sem):
    return pltpu.CompilerParams(dimension_semantics=sem, vmem_limit_bytes=VMEM_LIMIT_BYTES)


def _resident(shape):
    nd = len(shape)
    return pl.BlockSpec(shape, lambda *_: (0,) * nd, pipeline_mode=pl.Buffered(1))


def _mod_kernel(c_ref, w_ref, b_ref, o_ref):
    c = c_ref[...]
    a = c * _sigmoid(c)
    o_ref[...] = jnp.dot(a, w_ref[...], preferred_element_type=F32,
                         precision=lax.Precision.HIGHEST) + b_ref[...]


def _modulation(cvec, w_mod, b_mod):
    rows = cvec.shape[0]
    n = w_mod.shape[1]
    bn = D_MODEL
    return pl.pallas_call(
        _mod_kernel,
        grid=(n // bn,),
        in_specs=[pl.BlockSpec((rows, D_MODEL), lambda j: (0, 0)),
                  pl.BlockSpec((D_MODEL, bn), lambda j: (0, j)),
                  pl.BlockSpec((1, bn), lambda j: (0, j))],
        out_specs=pl.BlockSpec((rows, bn), lambda j: (0, j)),
        out_shape=jax.ShapeDtypeStruct((rows, n), F32),
        compiler_params=_params("arbitrary"),
        name="modulation",
    )(cvec, w_mod, b_mod.reshape(1, n))


def _ffn_kernel(x_ref, mod_ref, npre_ref, npost_ref, wi_ref, wo_ref, o_ref, *, j0, nchunk):
    x = x_ref[0]
    sh = mod_ref[0, j0:j0 + 1, :]
    sc = mod_ref[0, j0 + 1:j0 + 2, :]
    g = mod_ref[0, j0 + 2:j0 + 3, :]
    h = (_rms_rows(x, npre_ref[...]) * (1.0 + sc) + sh).astype(BF16)
    fc = D_FF // nchunk
    acc = None
    for c in range(nchunk):
        a = _dot(h, wi_ref[:, c * fc:(c + 1) * fc])
        b = _dot(h, wi_ref[:, D_FF + c * fc:D_FF + (c + 1) * fc])
        act = (a * _sigmoid(a) * b).astype(BF16)
        part = _dot(act, wo_ref[c * fc:(c + 1) * fc, :])
        acc = part if acc is None else acc + part
    o_ref[0] = x + 0.5 * g * _rms_rows(acc, npost_ref[...])


def _ffn(x, mod, npre, npost, wi, wo, *, j0, tm, per_batch_mod):
    B, T, _ = x.shape
    mod_idx = (lambda b, i: (b, 0, 0)) if per_batch_mod else (lambda b, i: (0, 0, 0))
    return pl.pallas_call(
        functools.partial(_ffn_kernel, j0=j0, nchunk=2),
        grid=(B, T // tm),
        in_specs=[pl.BlockSpec((1, tm, D_MODEL), lambda b, i: (b, i, 0)),
                  pl.BlockSpec((1, N_MOD, D_MODEL), mod_idx),
                  _resident((1, D_MODEL)), _resident((1, D_MODEL)),
                  _resident(wi.shape), _resident(wo.shape)],
        out_specs=pl.BlockSpec((1, tm, D_MODEL), lambda b, i: (b, i, 0)),
        out_shape=jax.ShapeDtypeStruct(x.shape, F32),
        compiler_params=_params("parallel", "parallel"),
        name="ffn_half_step",
    )(x, mod, npre.reshape(1, D_MODEL), npost.reshape(1, D_MODEL), wi, wo)


_NAT_MQ, _NAT_MV, _NAT_MO, _NAT_BG, _NAT_END = 0, 512, 1536, 2560, 4608
_TR_MK, _TR_AQ, _TR_AK, _TR_AV, _TR_MG, _TR_END = 0, 512, 1536, 1792, 2048, 2080


def _rope_swap(x):
    return jnp.concatenate([x[16:32], x[0:16], x[48:64], x[32:48]], axis=0)


def _proj_kernel(*refs, rope):
    if rope:
        (x_ref, mod_ref, npre_ref, wnat_ref, wtr_ref, gq_ref, gk_ref, cos_ref, sin_ref,
         mq_ref, mv_ref, mo_ref, bg_ref, mkT_ref, aqT_ref, akT_ref, avT_ref, mgT_ref) = refs
    else:
        (x_ref, mod_ref, npre_ref, wnat_ref, wtr_ref, gq_ref, gk_ref,
         mq_ref, mv_ref, mo_ref, bg_ref, mkT_ref, aqT_ref, akT_ref, avT_ref, mgT_ref) = refs
    x = x_ref[0]
    sh = mod_ref[0, 3:4, :]
    sc = mod_ref[0, 4:5, :]
    h = (_rms_rows(x, npre_ref[...]) * (1.0 + sc) + sh).astype(BF16)

    mq_ref[0] = (_dot(h, wnat_ref[:, _NAT_MQ:_NAT_MV]) * (M_DK ** -0.5)).astype(BF16)
    mv_ref[0] = _dot(h, wnat_ref[:, _NAT_MV:_NAT_MO]).astype(BF16)
    mo_ref[0] = _dot(h, wnat_ref[:, _NAT_MO:_NAT_BG]).astype(BF16)
    half = (_NAT_END - _NAT_BG) // 2
    bg_ref[0, :, 0:half] = _dot(h, wnat_ref[:, _NAT_BG:_NAT_BG + half]).astype(BF16)
    bg_ref[0, :, half:2 * half] = _dot(h, wnat_ref[:, _NAT_BG + half:_NAT_END]).astype(BF16)

    mkT_ref[0] = _dot_nt(wtr_ref[_TR_MK:_TR_AQ, :], h).astype(BF16)
    avT_ref[0] = _dot_nt(wtr_ref[_TR_AV:_TR_MG, :], h)
    mgT_ref[0] = _dot_nt(wtr_ref[_TR_MG:_TR_END, :], h)

    def head_norm(xh, gain_col):
        ms = jnp.mean(xh * xh, axis=0, keepdims=True)
        xh = xh * lax.rsqrt(ms + EPS) * gain_col
        if rope:
            xh = xh * cos_ref[...] + _rope_swap(xh) * sin_ref[...]
        return xh

    aqT = _dot_nt(wtr_ref[_TR_AQ:_TR_AK, :], h)
    for hh in range(A_HEADS):
        r0 = hh * HEAD_DIM
        xh = head_norm(aqT[r0:r0 + HEAD_DIM], gq_ref[...])
        aqT_ref[0, r0:r0 + HEAD_DIM, :] = (xh * (LOG2E * HEAD_DIM ** -0.5)).astype(BF16)
    akT = _dot_nt(wtr_ref[_TR_AK:_TR_AV, :], h)
    for hh in range(KV_HEADS):
        r0 = hh * HEAD_DIM
        akT_ref[0, r0:r0 + HEAD_DIM, :] = head_norm(akT[r0:r0 + HEAD_DIM], gk_ref[...])


def _projection(x, mod, npre, wnat, wtr, gq, gk, rope_tables, *, tm, per_batch_mod):
    B, T, _ = x.shape
    rope = rope_tables is not None
    mod_idx = (lambda b, i: (b, 0, 0)) if per_batch_mod else (lambda b, i: (0, 0, 0))
    tok = lambda n: pl.BlockSpec((1, tm, n), lambda b, i: (b, i, 0))
    feat = lambda n: pl.BlockSpec((1, n, tm), lambda b, i: (b, 0, i))
    in_specs = [tok(D_MODEL), pl.BlockSpec((1, N_MOD, D_MODEL), mod_idx),
                _resident((1, D_MODEL)), _resident(wnat.shape), _resident(wtr.shape),
                _resident((HEAD_DIM, 1)), _resident((HEAD_DIM, 1))]
    args = [x, mod, npre.reshape(1, D_MODEL), wnat, wtr, gq, gk]
    if rope:
        in_specs += [pl.BlockSpec((HEAD_DIM, tm), lambda b, i: (0, i))] * 2
        args += list(rope_tables)
    out_shape = [jax.ShapeDtypeStruct((B, T, 512), BF16),
                 jax.ShapeDtypeStruct((B, T, 1024), BF16),
                 jax.ShapeDtypeStruct((B, T, 1024), BF16),
                 jax.ShapeDtypeStruct((B, T, 2048), BF16),
                 jax.ShapeDtypeStruct((B, 512, T), BF16),
                 jax.ShapeDtypeStruct((B, 1024, T), BF16),
                 jax.ShapeDtypeStruct((B, 256, T), F32),
                 jax.ShapeDtypeStruct((B, 256, T), F32),
                 jax.ShapeDtypeStruct((B, 32, T), F32)]
    out_specs = [tok(512), tok(1024), tok(1024), tok(2048),
                 feat(512), feat(1024), feat(256), feat(256), feat(32)]
    return pl.pallas_call(
        functools.partial(_proj_kernel, rope=rope),
        grid=(B, T // tm),
        in_specs=in_specs, out_specs=out_specs, out_shape=out_shape,
        compiler_params=_params("parallel", "parallel"),
        name="mixer_in_proj",
    )(*args)


def _log_sigmoid(x):
    return jnp.minimum(x, 0.0) - jnp.log1p(jnp.exp(-jnp.abs(x)))


def _gate_kernel(g_ref, bias_ref, rows_ref, stats_ref, *, L):
    g = g_ref[0] + bias_ref[...]
    T = g.shape[1]
    lane = lax.broadcasted_iota(jnp.int32, (M_HEADS, T), 1) & (L - 1)
    for d in range(2):
        li = g[16 * d:16 * d + 8]
        lf = _log_sigmoid(g[16 * d + 8:16 * d + 16])

        def scan(v, op, fill):
            k = 1
            while k < L:
                if d == 0:
                    sh, ok = pltpu.roll(v, k, 1), lane >= k
                else:
                    sh, ok = pltpu.roll(v, T - k, 1), lane < L - k
                v = op(v, jnp.where(ok, sh, fill))
                k *= 2
            return v

        cum = scan(lf, jnp.add, 0.0)
        a = li - cum
        cm = scan(a, jnp.maximum, -jnp.inf)
        rows_ref[0, d, 0:8, :] = a
        rows_ref[0, d, 8:16, :] = cm
        rows_ref[0, d, 16:24, :] = cum
        for c in range(T // L):
            e = c * L + (L - 1 if d == 0 else 0)
            stats_ref[0, d, c, 0:8, :] = jnp.broadcast_to(cum[:, e:e + 1], (M_HEADS, 128))
            stats_ref[0, d, c, 8:16, :] = jnp.broadcast_to(cm[:, e:e + 1], (M_HEADS, 128))


def _gate_scan(mgT, bias_col, *, L):
    B, _, T = mgT.shape
    nc = T // L
    return pl.pallas_call(
        functools.partial(_gate_kernel, L=L),
        grid=(B,),
        in_specs=[pl.BlockSpec((1, 32, T), lambda b: (b, 0, 0)), _resident((32, 1))],
        out_specs=[pl.BlockSpec((1, 2, 24, T), lambda b: (b, 0, 0, 0)),
                   pl.BlockSpec((1, 2, nc, 16, 128), lambda b: (b, 0, 0, 0, 0))],
        out_shape=[jax.ShapeDtypeStruct((B, 2, 24, T), F32),
                   jax.ShapeDtypeStruct((B, 2, nc, 16, 128), F32)],
        compiler_params=_params("parallel"),
        name="mlstm_gate_scan",
    )(mgT, bias_col)


def _mlstm_kernel(qf_ref, qb_ref, kf_ref, kb_ref, vf_ref, vb_ref, rf_ref, rb_ref, cf_ref, cb_ref,
                  sf_ref, sb_ref, c0_ref, m0_ref, hf_ref, hb_ref, cst_ref, mst_ref, *, L, group):
    j = pl.program_id(1)

    @pl.when(j == 0)
    def _():
        cst_ref[...] = c0_ref[...]
        mst_ref[...] = m0_ref[...]

    row_i = lax.broadcasted_iota(jnp.int32, (L, L), 0)
    col_i = lax.broadcasted_iota(jnp.int32, (L, L), 1)
    lane = lax.broadcasted_iota(jnp.int32, (L, 128), 1)
    ones_blk = jnp.ones((L, M_DV), BF16)
    dirs = ((qf_ref, kf_ref, vf_ref, rf_ref, cf_ref, sf_ref, hf_ref),
            (qb_ref, kb_ref, vb_ref, rb_ref, cb_ref, sb_ref, hb_ref))
    per_dir = []
    for d, refs in enumerate(dirs):
        s_ref = refs[5]
        tot = s_ref[0, 0, 0, 0:8, :]
        amax = s_ref[0, 0, 0, 8:16, :]
        m_old = mst_ref[0, d]
        m_last = jnp.maximum(m_old, amax)
        a_prev = jnp.exp(m_old - m_last)
        mst_ref[0, d] = tot + m_last
        mask = (col_i <= row_i) if d == 0 else (col_i >= row_i)
        per_dir.append((mask, m_old, m_last, a_prev, refs[4][0, 0]))

    units = [(d, h) for d in range(2) for h in range(M_HEADS)]
    for g0 in range(0, len(units), group):
        grp = units[g0:g0 + group]
        st = []
        for d, h in grp:
            q_ref, k_ref, v_ref = dirs[d][0:3]
            p, e = divmod(h, 2)
            q_pair = q_ref[0, :, 128 * p:128 * p + 128]
            keep = (lane < M_DK) if e == 0 else (lane >= M_DK)
            qm = jnp.where(keep, q_pair, jnp.zeros_like(q_pair))
            c_pair = cst_ref[0, d, p]
            st.append(dict(
                s=_dot(qm, k_ref[0, 128 * p:128 * p + 128, :]),
                inter=_dot(qm, c_pair.astype(BF16)),
                c_own=c_pair[M_DK * e:M_DK * e + M_DK],
                v_aug=jnp.concatenate([v_ref[0, :, M_DV * h:M_DV * h + M_DV], ones_blk], axis=1)))
        for u, (d, h) in zip(st, grp):
            mask, m_old, m_last, a_prev, cols = per_dir[d]
            a_row = dirs[d][3][0, 0, h:h + 1, :]
            m_h = m_old[h:h + 1, :]
            big_m = jnp.maximum(m_h, cols[:, 8 + h:9 + h])
            u["decay"] = jnp.exp(jnp.where(mask, a_row - big_m, -jnp.inf))
            u["w"] = jnp.exp(m_h - big_m)
            u["floor"] = jnp.exp(-(cols[:, 16 + h:17 + h] + big_m))
            u["kscale"] = jnp.exp(a_row - m_last[h:h + 1, :])
        for u in st:
            u["intra"] = _dot((u["s"] * u["decay"]).astype(BF16), u["v_aug"])
        for u, (d, h) in zip(st, grp):
            w = u["w"]
            num_aug = jnp.concatenate([w, w], axis=1) * u["inter"] + u["intra"]
            den = num_aug[:, M_DV:2 * M_DV]
            dirs[d][6][0, :, M_DV * h:M_DV * h + M_DV] = (
                num_aug[:, 0:M_DV] / jnp.maximum(jnp.abs(den), u["floor"]))
        for u, (d, h) in zip(st, grp):
            p, e = divmod(h, 2)
            k_h = dirs[d][1][0, M_DK * h:M_DK * h + M_DK, :]
            k_scaled = (k_h.astype(F32) * u["kscale"]).astype(BF16)
            ap = per_dir[d][3][h:h + 1, :]
            cst_ref[0, d, p, M_DK * e:M_DK * e + M_DK, :] = (
                jnp.concatenate([ap, ap], axis=1) * u["c_own"] + _dot(k_scaled, u["v_aug"]))


def _mlstm(mq, mkT, mv, rows, cols, stats, c0aug, m0rep, *, L):
    assert L == 128, "decay matrices are built as whole (L, 128) vregs"
    B, T, _ = mq.shape
    nc = T // L
    fwd = lambda b, j: j
    bwd = lambda b, j: nc - 1 - j

    def tok(n, cidx):
        return pl.BlockSpec((1, L, n), lambda b, j: (b, cidx(b, j), 0))

    def feat(n, cidx):
        return pl.BlockSpec((1, n, L), lambda b, j: (b, 0, cidx(b, j)))

    def rows_spec(d, cidx):
        return pl.BlockSpec((1, 1, 24, L), lambda b, j: (b, d, 0, cidx(b, j)))

    def cols_spec(d, cidx):
        return pl.BlockSpec((1, 1, L, 24), lambda b, j: (b, d, cidx(b, j), 0))

    def stats_spec(d, cidx):
        return pl.BlockSpec((1, 1, 1, 16, 128), lambda b, j: (b, d, cidx(b, j), 0, 0))

    state_c = pl.BlockSpec((1, 2, 4, 128, 256), lambda b, j: (b, 0, 0, 0, 0))
    state_m = pl.BlockSpec((1, 2, 8, 128), lambda b, j: (b, 0, 0, 0))
    return pl.pallas_call(
        functools.partial(_mlstm_kernel, L=L, group=4),
        grid=(B, nc),
        in_specs=[tok(512, fwd), tok(512, bwd), feat(512, fwd), feat(512, bwd),
                  tok(1024, fwd), tok(1024, bwd),
                  rows_spec(0, fwd), rows_spec(1, bwd), cols_spec(0, fwd), cols_spec(1, bwd),
                  stats_spec(0, fwd), stats_spec(1, bwd), state_c, state_m],
        out_specs=[tok(1024, fwd), tok(1024, bwd), state_c, state_m],
        out_shape=[jax.ShapeDtypeStruct((B, T, 1024), F32),
                   jax.ShapeDtypeStruct((B, T, 1024), F32),
                   jax.ShapeDtypeStruct((B, 2, 4, 128, 256), F32),
                   jax.ShapeDtypeStruct((B, 2, 8, 128), F32)],
        compiler_params=_params("parallel", "arbitrary"),
        name="mlstm_chunks",
    )(mq, mq, mkT, mkT, mv, mv, rows, rows, cols, cols, stats, stats, c0aug, m0rep)


def _attn_kernel(q_ref, k_ref, vT_ref, o_ref, s_scr, m_scr, acc_scr, l_scr, *, tq, nq, kc, pc):
    i = pl.program_id(2)
    n = GROUP * tq
    S = k_ref.shape[2]

    def run(scores_slot, weigh_slot, emit_slot):
        if emit_slot is not None:
            l = jnp.sum(l_scr[emit_slot], axis=0, keepdims=True)
            oT = acc_scr[emit_slot] / l
            o_ref[0] = jnp.concatenate(
                [oT[:, g * tq:(g + 1) * tq].T for g in range(GROUP)], axis=1).astype(BF16)
        if scores_slot is not None:
            qT = jnp.concatenate([q_ref[0, g] for g in range(GROUP)], axis=1)
            pm = None
        if weigh_slot is not None:
            m = m_scr[weigh_slot][0:1, :]
            acc = lsum = None
        for c in range(S // pc):
            if scores_slot is not None:
                for r0 in range(c * pc, (c + 1) * pc, kc):
                    s_c = _dot(k_ref[0, 0, r0:r0 + kc, :], qT)
                    s_scr[scores_slot, r0:r0 + kc, :] = s_c
                    part = jnp.max(s_c.reshape(kc // 8, 8, n), axis=0)
                    pm = part if pm is None else jnp.maximum(pm, part)
            if weigh_slot is not None:
                p = jnp.exp2(s_scr[weigh_slot, c * pc:(c + 1) * pc, :] - m)
                psum = jnp.sum(p.reshape(pc // 8, 8, n), axis=0)
                lsum = psum if lsum is None else lsum + psum
                part = _dot(vT_ref[0, 0, :, c * pc:(c + 1) * pc], p.astype(BF16))
                acc = part if acc is None else acc + part
        if scores_slot is not None:
            m_scr[scores_slot] = jnp.broadcast_to(jnp.max(pm, axis=0, keepdims=True), (8, n))
        if weigh_slot is not None:
            acc_scr[weigh_slot] = acc
            l_scr[weigh_slot] = lsum

    if nq == 1:
        run(0, None, None)
        run(None, 0, None)
        run(None, None, 0)
        return

    @pl.when(i == 0)
    def _():
        run(0, None, None)

    @pl.when(i == 1)
    def _():
        run(1, 0, None)

    for cur in range(2):
        @pl.when(jnp.logical_and(jnp.logical_and(i >= 2, i < nq), i % 2 == cur))
        def _():
            run(cur, 1 - cur, cur)

    @pl.when(i == nq)
    def _():
        run(None, 1 - nq % 2, nq % 2)

    @pl.when(i == nq + 1)
    def _():
        run(None, None, (nq + 1) % 2)


def _attention(aqT, k, vT, *, tq):
    B, _, T = aqT.shape
    S = k.shape[2]
    nq = T // tq
    n = GROUP * tq
    steps = 1 if nq == 1 else nq + 2
    q4 = aqT.reshape(B, A_HEADS, HEAD_DIM, T)
    return pl.pallas_call(
        functools.partial(_attn_kernel, tq=tq, nq=nq, kc=256, pc=256),
        grid=(B, KV_HEADS, steps),
        in_specs=[pl.BlockSpec((1, GROUP, HEAD_DIM, tq),
                               lambda b, kv, i: (b, kv, 0, jnp.minimum(i, nq - 1))),
                  pl.BlockSpec((1, 1, S, HEAD_DIM), lambda b, kv, i: (b, kv, 0, 0)),
                  pl.BlockSpec((1, 1, HEAD_DIM, S), lambda b, kv, i: (b, kv, 0, 0))],
        out_specs=pl.BlockSpec((1, tq, GROUP * HEAD_DIM),
                               lambda b, kv, i: (b, jnp.clip(i - 2, 0, nq - 1), kv)),
        out_shape=jax.ShapeDtypeStruct((B, T, A_HEADS * HEAD_DIM), BF16),
        scratch_shapes=[pltpu.VMEM((2, S, n), F32), pltpu.VMEM((2, 8, n), F32),
                        pltpu.VMEM((2, HEAD_DIM, n), F32), pltpu.VMEM((2, 8, n), F32)],
        compiler_params=_params("parallel", "parallel", "arbitrary"),
        name="block_attention",
    )(q4, k, vT)


def _merge_kernel(x_ref, mod_ref, hf_ref, hb_ref, mo_ref, ha_ref, bg_ref, mn_ref, npost_ref,
                  wbm_ref, wba_ref, wout_ref, o_ref):
    hm = hf_ref[0] + hb_ref[0]
    parts = []
    for h in range(M_HEADS):
        xh = hm[:, M_DV * h:M_DV * (h + 1)]
        ms = jnp.mean(xh * xh, axis=-1, keepdims=True)
        parts.append(xh * lax.rsqrt(ms + EPS))
    hn = jnp.concatenate(parts, axis=1) * mn_ref[...]
    hmo = (hn * _sigmoid(mo_ref[0].astype(F32))).astype(BF16)
    ym = _dot(hmo, wbm_ref[...])
    ya = _dot(ha_ref[0], wba_ref[...])
    bg = bg_ref[0].astype(F32)
    y = _sigmoid(bg[:, 0:D_MODEL]) * ym + _sigmoid(bg[:, D_MODEL:2 * D_MODEL]) * ya
    yo = _dot(y.astype(BF16), wout_ref[...])
    g1 = mod_ref[0, 5:6, :]
    o_ref[0] = x_ref[0] + g1 * _rms_rows(yo, npost_ref[...])


def _merge(x, mod, hf, hb, mo, ha, bg, mnorm, npost, wbm, wba, wout, *, tm, per_batch_mod):
    B, T, _ = x.shape
    mod_idx = (lambda b, i: (b, 0, 0)) if per_batch_mod else (lambda b, i: (0, 0, 0))
    tok = lambda n: pl.BlockSpec((1, tm, n), lambda b, i: (b, i, 0))
    return pl.pallas_call(
        _merge_kernel,
        grid=(B, T // tm),
        in_specs=[tok(D_MODEL), pl.BlockSpec((1, N_MOD, D_MODEL), mod_idx),
                  tok(1024), tok(1024), tok(1024), tok(1024), tok(2048),
                  _resident((1, 1024)), _resident((1, D_MODEL)),
                  _resident(wbm.shape), _resident(wba.shape), _resident(wout.shape)],
        out_specs=tok(D_MODEL),
        out_shape=jax.ShapeDtypeStruct(x.shape, F32),
        compiler_params=_params("parallel", "parallel"),
        name="mixer_merge",
    )(x, mod, hf, hb, mo, ha, bg, mnorm.reshape(1, 1024), npost.reshape(1, D_MODEL), wbm, wba, wout)


def _rope_tables(T):
    pairs = HEAD_DIM // 4
    t = jnp.arange(T)
    pos = jnp.stack([t // GRID_W, t % GRID_W], axis=0).astype(F32)
    inv_freq = ROPE_THETA ** (-jnp.arange(pairs, dtype=F32) / pairs)
    ang = pos[:, None, :] * inv_freq[None, :, None]
    cos, sin = jnp.cos(ang), jnp.sin(ang)
    cos_t = jnp.concatenate([cos[0], cos[0], cos[1], cos[1]], axis=0)
    sin_t = jnp.concatenate([-sin[0], sin[0], -sin[1], sin[1]], axis=0)
    return cos_t, sin_t


def _trunk(x, mod, per_batch_mod, w, ctx, *, tm):
    B, T, _ = x.shape
    L = MCHUNK
    x = _ffn(x, mod, w["npre"][0], w["npost"][0], w["wi"][0], w["wo"][0],
             j0=0, tm=tm, per_batch_mod=per_batch_mod)
    rope_tables = _rope_tables(T) if ctx is not None else None
    mq, mv, mo, bg, mkT, aqT, akT, avT, mgT = _projection(
        x, mod, w["npre"][1], w["wnat"], w["wtr"], w["gq"], w["gk"], rope_tables,
        tm=tm, per_batch_mod=per_batch_mod)

    rows, stats = _gate_scan(mgT, w["gate_bias"], L=L)
    cols = jnp.swapaxes(rows, 2, 3)
    if ctx is None:
        c0aug = jnp.zeros((B, 2, 4, 128, 256), F32)
        m0rep = jnp.zeros((B, 2, M_HEADS, 128), F32)
    else:
        c0, n0, m0 = ctx[2], ctx[3], ctx[4]
        n0rep = jnp.broadcast_to(n0[..., None], c0.shape)
        c0aug = jnp.concatenate([c0, n0rep], axis=-1).reshape(B, 2, 4, 128, 256)
        m0rep = jnp.broadcast_to(m0[..., None], (B, 2, M_HEADS, 128))
    hf, hb, c_fin, m_fin = _mlstm(mq, mkT, mv, rows, cols, stats, c0aug, m0rep, L=L)

    k_tok = jnp.swapaxes(akT.reshape(B, KV_HEADS, HEAD_DIM, T), 2, 3).astype(BF16)
    vT = avT.reshape(B, KV_HEADS, HEAD_DIM, T).astype(BF16)
    if ctx is not None:
        k_ctx = jnp.transpose(ctx[0], (0, 2, 1, 3)).astype(BF16)
        v_ctx = jnp.transpose(ctx[1], (0, 2, 3, 1)).astype(BF16)
        k_tok = jnp.concatenate([k_tok, k_ctx], axis=2)
        vT = jnp.concatenate([vT, v_ctx], axis=3)
    ha = _attention(aqT, k_tok, vT, tq=128 if ctx is not None else T)

    x = _merge(x, mod, hf, hb, mo, ha, bg, w["mnorm"], w["npost"][1], w["wbm"], w["wba"], w["wout"],
               tm=tm, per_batch_mod=per_batch_mod)
    x = _ffn(x, mod, w["npre"][2], w["npost"][2], w["wi"][1], w["wo"][1],
             j0=6, tm=tm, per_batch_mod=per_batch_mod)
    return x, (akT, avT, c_fin, m_fin)


def _layer_weights(norm_pre, norm_post, ffn_w_in, ffn_w_out, w_in_mix, gate_bias, mlstm_norm,
                   qk_norm, w_branch_m, w_branch_a, w_out):
    o = [0]
    for s in MIX_SIZES:
        o.append(o[-1] + s)
    col = lambda i: w_in_mix[:, o[i]:o[i + 1]]
    wnat = jnp.concatenate([col(0), col(2), col(3), col(8)], axis=1).astype(BF16)
    wtr = jnp.concatenate([col(1), col(5), col(6), col(7), col(4)], axis=1).T.astype(BF16)
    return dict(
        npre=norm_pre, npost=norm_post,
        wi=ffn_w_in.astype(BF16), wo=ffn_w_out.astype(BF16),
        wnat=wnat, wtr=wtr,
        gq=qk_norm[0].reshape(HEAD_DIM, 1), gk=qk_norm[1].reshape(HEAD_DIM, 1),
        gate_bias=gate_bias.reshape(32, 1), mnorm=mlstm_norm,
        wbm=w_branch_m.astype(BF16), wba=w_branch_a.astype(BF16), wout=w_out.astype(BF16))


def kernel(x_prompt, x_sample, c, cache_k, cache_v, state_C, state_n, state_m, c_ctx, w_mod, b_mod,
           norm_pre, norm_post, ffn_w_in, ffn_w_out, w_in_mix, mlstm_gate_bias, mlstm_norm, qk_norm,
           w_branch_m, w_branch_a, w_out):
    depth = w_mod.shape[0]
    nb = c.shape[0]
    y_prompt, y_sample = x_prompt, x_sample
    ks, vs, cs, ns, ms = [], [], [], [], []
    for l in range(depth):
        w = _layer_weights(norm_pre[l], norm_post[l], ffn_w_in[l], ffn_w_out[l], w_in_mix[l],
                           mlstm_gate_bias[l], mlstm_norm[l], qk_norm[l], w_branch_m[l],
                           w_branch_a[l], w_out[l])
        cvec = jnp.concatenate([c_ctx[None, :], c, jnp.zeros((16 - 1 - nb, D_MODEL), F32)], axis=0)
        mod = _modulation(cvec, w_mod[l], b_mod[l]).reshape(16, N_MOD, D_MODEL)
        y_prompt, (akT, avT, c_fin, m_fin) = _trunk(y_prompt, mod[0:1], False, w, None, tm=256)
        bp, tp = x_prompt.shape[0], x_prompt.shape[1]
        ks.append(jnp.swapaxes(akT, 1, 2).reshape(bp, tp, KV_HEADS, HEAD_DIM))
        vs.append(jnp.swapaxes(avT, 1, 2).reshape(bp, tp, KV_HEADS, HEAD_DIM))
        c_fin = c_fin.reshape(bp, 2, M_HEADS, M_DK, 2 * M_DV)
        cs.append(c_fin[..., 0:M_DV])
        ns.append(c_fin[..., M_DV])
        ms.append(m_fin[..., 0])
        ctx = (cache_k[:, l], cache_v[:, l], state_C[:, l], state_n[:, l], state_m[:, l])
        y_sample, _ = _trunk(y_sample, mod[1:1 + nb], True, w, ctx, tm=512)
    dt = x_prompt.dtype
    return (y_prompt, y_sample, jnp.stack(ks, axis=1), jnp.stack(vs, axis=1),
            jnp.stack(cs, axis=1).astype(dt), jnp.stack(ns, axis=1).astype(dt),
            jnp.stack(ms, axis=1).astype(dt))
```

```python
import functools

import jax
import jax.numpy as jnp
from jax import lax
from jax.experimental import pallas as pl
from jax.experimental.pallas import tpu as pltpu

F32 = jnp.float32
BF16 = jnp.bfloat16

D_MODEL = 1024
D_FF = 2816
M_HEADS = 8
M_DK = 64
M_DV = 128
A_HEADS = 16
KV_HEADS = 4
GROUP = A_HEADS // KV_HEADS
HEAD_DIM = 64
GRID_W = 64
ROPE_THETA = 10000.0
N_MOD = 9
EPS = 1e-6
LOG2E = 1.4426950408889634
MIX_SIZES = (512, 512, 1024, 1024, 32, 1024, 256, 256, 2048)

MCHUNK = 128
VMEM_LIMIT_BYTES = 56 * 1024 * 1024

_NT_DIMS = (((1,), (1,)), ((), ()))


def _dot(a, b):
    return jnp.dot(a, b, preferred_element_type=F32)


def _dot_nt(a, b):
    return lax.dot_general(a, b, _NT_DIMS, preferred_element_type=F32)


def _rms_rows(x, gain_row):
    ms = jnp.mean(x * x, axis=-1, keepdims=True)
    return x * lax.rsqrt(ms + EPS) * gain_row


def _sigmoid(x):
    return jax.nn.sigmoid(x)


def _params(*sem):
    return pltpu.CompilerParams(dimension_semantics=sem, vmem_limit_bytes=VMEM_LIMIT_BYTES)


def _resident(shape):
    nd = len(shape)
    return pl.BlockSpec(shape, lambda *_: (0,) * nd, pipeline_mode=pl.Buffered(1))


def _mod_kernel(c_ref, w_ref, b_ref, o_ref):
    c = c_ref[...]
    a = c * _sigmoid(c)
    o_ref[...] = jnp.dot(a, w_ref[...], preferred_element_type=F32,
                         precision=lax.Precision.HIGHEST) + b_ref[...]


def _modulation(cvec, w_mod, b_mod):
    rows = cvec.shape[0]
    n = w_mod.shape[1]
    bn = D_MODEL
    return pl.pallas_call(
        _mod_kernel,
        grid=(n // bn,),
        in_specs=[pl.BlockSpec((rows, D_MODEL), lambda j: (0, 0)),
                  pl.BlockSpec((D_MODEL, bn), lambda j: (0, j)),
                  pl.BlockSpec((1, bn), lambda j: (0, j))],
        out_specs=pl.BlockSpec((rows, bn), lambda j: (0, j)),
        out_shape=jax.ShapeDtypeStruct((rows, n), F32),
        compiler_params=_params("arbitrary"),
        name="modulation",
    )(cvec, w_mod, b_mod.reshape(1, n))


FFN_BOUNDS = (0, 1536, D_FF)


def _ffn_kernel(x_ref, mod_ref, npre_ref, npost_ref, wi_ref, wo_ref, o_ref, *, j0, bounds):
    x = x_ref[0]
    sh = mod_ref[0, j0:j0 + 1, :]
    sc = mod_ref[0, j0 + 1:j0 + 2, :]
    g = mod_ref[0, j0 + 2:j0 + 3, :]
    h = (_rms_rows(x, npre_ref[...]) * (1.0 + sc) + sh).astype(BF16)
    acc = None
    for c0, c1 in zip(bounds[:-1], bounds[1:]):
        a = _dot(h, wi_ref[:, c0:c1])
        b = _dot(h, wi_ref[:, D_FF + c0:D_FF + c1])
        act = (a * _sigmoid(a) * b).astype(BF16)
        part = _dot(act, wo_ref[c0:c1, :])
        acc = part if acc is None else acc + part
    o_ref[0] = x + 0.5 * g * _rms_rows(acc, npost_ref[...])


def _ffn(x, mod, npre, npost, wi, wo, *, j0, tm, per_batch_mod):
    B, T, _ = x.shape
    mod_idx = (lambda b, i: (b, 0, 0)) if per_batch_mod else (lambda b, i: (0, 0, 0))
    return pl.pallas_call(
        functools.partial(_ffn_kernel, j0=j0, bounds=FFN_BOUNDS),
        grid=(B, T // tm),
        in_specs=[pl.BlockSpec((1, tm, D_MODEL), lambda b, i: (b, i, 0)),
                  pl.BlockSpec((1, N_MOD, D_MODEL), mod_idx),
                  _resident((1, D_MODEL)), _resident((1, D_MODEL)),
                  _resident(wi.shape), _resident(wo.shape)],
        out_specs=pl.BlockSpec((1, tm, D_MODEL), lambda b, i: (b, i, 0)),
        out_shape=jax.ShapeDtypeStruct(x.shape, F32),
        compiler_params=_params("parallel", "parallel"),
        name="ffn_half_step",
    )(x, mod, npre.reshape(1, D_MODEL), npost.reshape(1, D_MODEL), wi, wo)


_NAT_MQ, _NAT_MV, _NAT_MO, _NAT_BG, _NAT_END = 0, 512, 1536, 2560, 4608
_TR_MK, _TR_AQ, _TR_AK, _TR_AV, _TR_MG, _TR_END = 0, 512, 1536, 1792, 2048, 2080


def _rope_swap(x):
    return jnp.concatenate([x[16:32], x[0:16], x[48:64], x[32:48]], axis=0)


def _proj_kernel(*refs, rope):
    if rope:
        (x_ref, mod_ref, npre_ref, wnat_ref, wtr_ref, gq_ref, gk_ref, cos_ref, sin_ref,
         mq_ref, mv_ref, mo_ref, bg_ref, mkT_ref, aqT_ref, akT_ref, avT_ref, mgT_ref) = refs
    else:
        (x_ref, mod_ref, npre_ref, wnat_ref, wtr_ref, gq_ref, gk_ref,
         mq_ref, mv_ref, mo_ref, bg_ref, mkT_ref, aqT_ref, akT_ref, avT_ref, mgT_ref) = refs
    x = x_ref[0]
    sh = mod_ref[0, 3:4, :]
    sc = mod_ref[0, 4:5, :]
    h = (_rms_rows(x, npre_ref[...]) * (1.0 + sc) + sh).astype(BF16)

    mq_ref[0] = (_dot(h, wnat_ref[:, _NAT_MQ:_NAT_MV]) * (M_DK ** -0.5)).astype(BF16)
    mv_ref[0] = _dot(h, wnat_ref[:, _NAT_MV:_NAT_MO]).astype(BF16)
    mo_ref[0] = _dot(h, wnat_ref[:, _NAT_MO:_NAT_BG]).astype(BF16)
    half = (_NAT_END - _NAT_BG) // 2
    bg_ref[0, :, 0:half] = _dot(h, wnat_ref[:, _NAT_BG:_NAT_BG + half]).astype(BF16)
    bg_ref[0, :, half:2 * half] = _dot(h, wnat_ref[:, _NAT_BG + half:_NAT_END]).astype(BF16)

    mkT_ref[0] = _dot_nt(wtr_ref[_TR_MK:_TR_AQ, :], h).astype(BF16)
    avT_ref[0] = _dot_nt(wtr_ref[_TR_AV:_TR_MG, :], h)
    mgT_ref[0] = _dot_nt(wtr_ref[_TR_MG:_TR_END, :], h)

    def head_norm(xh, gain_col):
        ms = jnp.mean(xh * xh, axis=0, keepdims=True)
        xh = xh * lax.rsqrt(ms + EPS) * gain_col
        if rope:
            xh = xh * cos_ref[...] + _rope_swap(xh) * sin_ref[...]
        return xh

    aqT = _dot_nt(wtr_ref[_TR_AQ:_TR_AK, :], h)
    for hh in range(A_HEADS):
        r0 = hh * HEAD_DIM
        xh = head_norm(aqT[r0:r0 + HEAD_DIM], gq_ref[...])
        aqT_ref[0, r0:r0 + HEAD_DIM, :] = (xh * (LOG2E * HEAD_DIM ** -0.5)).astype(BF16)
    akT = _dot_nt(wtr_ref[_TR_AK:_TR_AV, :], h)
    for hh in range(KV_HEADS):
        r0 = hh * HEAD_DIM
        akT_ref[0, r0:r0 + HEAD_DIM, :] = head_norm(akT[r0:r0 + HEAD_DIM], gk_ref[...])


def _projection(x, mod, npre, wnat, wtr, gq, gk, rope_tables, *, tm, per_batch_mod):
    B, T, _ = x.shape
    rope = rope_tables is not None
    mod_idx = (lambda b, i: (b, 0, 0)) if per_batch_mod else (lambda b, i: (0, 0, 0))
    tok = lambda n: pl.BlockSpec((1, tm, n), lambda b, i: (b, i, 0))
    feat = lambda n: pl.BlockSpec((1, n, tm), lambda b, i: (b, 0, i))
    in_specs = [tok(D_MODEL), pl.BlockSpec((1, N_MOD, D_MODEL), mod_idx),
                _resident((1, D_MODEL)), _resident(wnat.shape), _resident(wtr.shape),
                _resident((HEAD_DIM, 1)), _resident((HEAD_DIM, 1))]
    args = [x, mod, npre.reshape(1, D_MODEL), wnat, wtr, gq, gk]
    if rope:
        in_specs += [pl.BlockSpec((HEAD_DIM, tm), lambda b, i: (0, i))] * 2
        args += list(rope_tables)
    out_shape = [jax.ShapeDtypeStruct((B, T, 512), BF16),
                 jax.ShapeDtypeStruct((B, T, 1024), BF16),
                 jax.ShapeDtypeStruct((B, T, 1024), BF16),
                 jax.ShapeDtypeStruct((B, T, 2048), BF16),
                 jax.ShapeDtypeStruct((B, 512, T), BF16),
                 jax.ShapeDtypeStruct((B, 1024, T), BF16),
                 jax.ShapeDtypeStruct((B, 256, T), F32),
                 jax.ShapeDtypeStruct((B, 256, T), F32),
                 jax.ShapeDtypeStruct((B, 32, T), F32)]
    out_specs = [tok(512), tok(1024), tok(1024), tok(2048),
                 feat(512), feat(1024), feat(256), feat(256), feat(32)]
    return pl.pallas_call(
        functools.partial(_proj_kernel, rope=rope),
        grid=(B, T // tm),
        in_specs=in_specs, out_specs=out_specs, out_shape=out_shape,
        compiler_params=_params("parallel", "parallel"),
        name="mixer_in_proj",
    )(*args)


def _log_sigmoid(x):
    return jnp.minimum(x, 0.0) - jnp.log1p(jnp.exp(-jnp.abs(x)))


def _gate_kernel(g_ref, bias_ref, rows_ref, stats_ref, *, L):
    g = g_ref[0] + bias_ref[...]
    T = g.shape[1]
    lane = lax.broadcasted_iota(jnp.int32, (M_HEADS, T), 1) & (L - 1)
    for d in range(2):
        li = g[16 * d:16 * d + 8]
        lf = _log_sigmoid(g[16 * d + 8:16 * d + 16])

        def scan(v, op, fill):
            k = 1
            while k < L:
                if d == 0:
                    sh, ok = pltpu.roll(v, k, 1), lane >= k
                else:
                    sh, ok = pltpu.roll(v, T - k, 1), lane < L - k
                v = op(v, jnp.where(ok, sh, fill))
                k *= 2
            return v

        cum = scan(lf, jnp.add, 0.0)
        a = li - cum
        cm = scan(a, jnp.maximum, -jnp.inf)
        rows_ref[0, d, 0:8, :] = a
        rows_ref[0, d, 8:16, :] = cm
        rows_ref[0, d, 16:24, :] = cum
        for c in range(T // L):
            e = c * L + (L - 1 if d == 0 else 0)
            stats_ref[0, d, c, 0:8, :] = jnp.broadcast_to(cum[:, e:e + 1], (M_HEADS, 128))
            stats_ref[0, d, c, 8:16, :] = jnp.broadcast_to(cm[:, e:e + 1], (M_HEADS, 128))


def _gate_scan(mgT, bias_col, *, L):
    B, _, T = mgT.shape
    nc = T // L
    return pl.pallas_call(
        functools.partial(_gate_kernel, L=L),
        grid=(B,),
        in_specs=[pl.BlockSpec((1, 32, T), lambda b: (b, 0, 0)), _resident((32, 1))],
        out_specs=[pl.BlockSpec((1, 2, 24, T), lambda b: (b, 0, 0, 0)),
                   pl.BlockSpec((1, 2, nc, 16, 128), lambda b: (b, 0, 0, 0, 0))],
        out_shape=[jax.ShapeDtypeStruct((B, 2, 24, T), F32),
                   jax.ShapeDtypeStruct((B, 2, nc, 16, 128), F32)],
        compiler_params=_params("parallel"),
        name="mlstm_gate_scan",
    )(mgT, bias_col)


def _mlstm_kernel(qf_ref, qb_ref, kf_ref, kb_ref, vf_ref, vb_ref, rf_ref, rb_ref, cf_ref, cb_ref,
                  sf_ref, sb_ref, c0_ref, m0_ref, hf_ref, hb_ref, cst_ref, mst_ref, *, L, group):
    j = pl.program_id(1)

    @pl.when(j == 0)
    def _():
        cst_ref[...] = c0_ref[...]
        mst_ref[...] = m0_ref[...]

    row_i = lax.broadcasted_iota(jnp.int32, (L, L), 0)
    col_i = lax.broadcasted_iota(jnp.int32, (L, L), 1)
    lane = lax.broadcasted_iota(jnp.int32, (L, 128), 1)
    ones_blk = jnp.ones((L, M_DV), BF16)
    dirs = ((qf_ref, kf_ref, vf_ref, rf_ref, cf_ref, sf_ref, hf_ref),
            (qb_ref, kb_ref, vb_ref, rb_ref, cb_ref, sb_ref, hb_ref))
    per_dir = []
    for d, refs in enumerate(dirs):
        s_ref = refs[5]
        tot = s_ref[0, 0, 0, 0:8, :]
        amax = s_ref[0, 0, 0, 8:16, :]
        m_old = mst_ref[0, d]
        m_last = jnp.maximum(m_old, amax)
        a_prev = jnp.exp(m_old - m_last)
        mst_ref[0, d] = tot + m_last
        mask = (col_i <= row_i) if d == 0 else (col_i >= row_i)
        per_dir.append((mask, m_old, m_last, a_prev, refs[4][0, 0]))

    units = [(d, h) for d in range(2) for h in range(M_HEADS)]
    for g0 in range(0, len(units), group):
        grp = units[g0:g0 + group]
        st = []
        for d, h in grp:
            q_ref, k_ref, v_ref = dirs[d][0:3]
            p, e = divmod(h, 2)
            q_pair = q_ref[0, :, 128 * p:128 * p + 128]
            keep = (lane < M_DK) if e == 0 else (lane >= M_DK)
            qm = jnp.where(keep, q_pair, jnp.zeros_like(q_pair))
            c_pair = cst_ref[0, d, p]
            st.append(dict(
                s=_dot(qm, k_ref[0, 128 * p:128 * p + 128, :]),
                inter=_dot(qm, c_pair.astype(BF16)),
                c_own=c_pair[M_DK * e:M_DK * e + M_DK],
                v_aug=jnp.concatenate([v_ref[0, :, M_DV * h:M_DV * h + M_DV], ones_blk], axis=1)))
        for u, (d, h) in zip(st, grp):
            mask, m_old, m_last, a_prev, cols = per_dir[d]
            a_row = dirs[d][3][0, 0, h:h + 1, :]
            m_h = m_old[h:h + 1, :]
            big_m = jnp.maximum(m_h, cols[:, 8 + h:9 + h])
            u["decay"] = jnp.exp(jnp.where(mask, a_row - big_m, -jnp.inf))
            u["w"] = jnp.exp(m_h - big_m)
            u["floor"] = jnp.exp(-(cols[:, 16 + h:17 + h] + big_m))
            u["kscale"] = jnp.exp(a_row - m_last[h:h + 1, :])
        for u in st:
            u["intra"] = _dot((u["s"] * u["decay"]).astype(BF16), u["v_aug"])
        for u, (d, h) in zip(st, grp):
            w = u["w"]
            num_aug = jnp.concatenate([w, w], axis=1) * u["inter"] + u["intra"]
            den = num_aug[:, M_DV:2 * M_DV]
            dirs[d][6][0, :, M_DV * h:M_DV * h + M_DV] = (
                num_aug[:, 0:M_DV] / jnp.maximum(jnp.abs(den), u["floor"]))
        for u, (d, h) in zip(st, grp):
            p, e = divmod(h, 2)
            k_h = dirs[d][1][0, M_DK * h:M_DK * h + M_DK, :]
            k_scaled = (k_h.astype(F32) * u["kscale"]).astype(BF16)
            ap = per_dir[d][3][h:h + 1, :]
            cst_ref[0, d, p, M_DK * e:M_DK * e + M_DK, :] = (
                jnp.concatenate([ap, ap], axis=1) * u["c_own"] + _dot(k_scaled, u["v_aug"]))


def _mlstm(mq, mkT, mv, rows, cols, stats, c0aug, m0rep, *, L):
    assert L == 128, "decay matrices are built as whole (L, 128) vregs"
    B, T, _ = mq.shape
    nc = T // L
    fwd = lambda b, j: j
    bwd = lambda b, j: nc - 1 - j

    def tok(n, cidx):
        return pl.BlockSpec((1, L, n), lambda b, j: (b, cidx(b, j), 0))

    def feat(n, cidx):
        return pl.BlockSpec((1, n, L), lambda b, j: (b, 0, cidx(b, j)))

    def rows_spec(d, cidx):
        return pl.BlockSpec((1, 1, 24, L), lambda b, j: (b, d, 0, cidx(b, j)))

    def cols_spec(d, cidx):
        return pl.BlockSpec((1, 1, L, 24), lambda b, j: (b, d, cidx(b, j), 0))

    def stats_spec(d, cidx):
        return pl.BlockSpec((1, 1, 1, 16, 128), lambda b, j: (b, d, cidx(b, j), 0, 0))

    state_c = pl.BlockSpec((1, 2, 4, 128, 256), lambda b, j: (b, 0, 0, 0, 0))
    state_m = pl.BlockSpec((1, 2, 8, 128), lambda b, j: (b, 0, 0, 0))
    return pl.pallas_call(
        functools.partial(_mlstm_kernel, L=L, group=4),
        grid=(B, nc),
        in_specs=[tok(512, fwd), tok(512, bwd), feat(512, fwd), feat(512, bwd),
                  tok(1024, fwd), tok(1024, bwd),
                  rows_spec(0, fwd), rows_spec(1, bwd), cols_spec(0, fwd), cols_spec(1, bwd),
                  stats_spec(0, fwd), stats_spec(1, bwd), state_c, state_m],
        out_specs=[tok(1024, fwd), tok(1024, bwd), state_c, state_m],
        out_shape=[jax.ShapeDtypeStruct((B, T, 1024), F32),
                   jax.ShapeDtypeStruct((B, T, 1024), F32),
                   jax.ShapeDtypeStruct((B, 2, 4, 128, 256), F32),
                   jax.ShapeDtypeStruct((B, 2, 8, 128), F32)],
        compiler_params=_params("parallel", "arbitrary"),
        name="mlstm_chunks",
    )(mq, mq, mkT, mkT, mv, mv, rows, rows, cols, cols, stats, stats, c0aug, m0rep)


def _attn_kernel(q_ref, k_ref, vT_ref, o_ref, s_scr, m_scr, acc_scr, l_scr, *, tq, nq, kc, pc):
    n = GROUP * tq
    S = k_ref.shape[2]

    def block_start(j):
        return j * tq if isinstance(j, int) else pl.multiple_of(j * tq, tq)

    def run(scores, weigh_slot, emit):
        if emit is not None:
            emit_slot, j = emit
            l = jnp.sum(l_scr[emit_slot], axis=0, keepdims=True)
            oT = acc_scr[emit_slot] / l
            o_ref[0, pl.ds(block_start(j), tq), :] = jnp.concatenate(
                [oT[:, g * tq:(g + 1) * tq].T for g in range(GROUP)], axis=1).astype(BF16)
        scores_slot = None
        if scores is not None:
            scores_slot, j = scores
            qT = jnp.concatenate([q_ref[0, g, :, pl.ds(block_start(j), tq)]
                                  for g in range(GROUP)], axis=1)
            pm = None
        if weigh_slot is not None:
            m = m_scr[weigh_slot][0:1, :]
            acc = lsum = None
        for c in range(S // pc):
            if scores_slot is not None:
                for r0 in range(c * pc, (c + 1) * pc, kc):
                    s_c = _dot(k_ref[0, 0, r0:r0 + kc, :], qT)
                    s_scr[scores_slot, r0:r0 + kc, :] = s_c
                    part = jnp.max(s_c.reshape(kc // 8, 8, n), axis=0)
                    pm = part if pm is None else jnp.maximum(pm, part)
            if weigh_slot is not None:
                p = jnp.exp2(s_scr[weigh_slot, c * pc:(c + 1) * pc, :] - m)
                psum = jnp.sum(p.reshape(pc // 8, 8, n), axis=0)
                lsum = psum if lsum is None else lsum + psum
                part = _dot(vT_ref[0, 0, :, c * pc:(c + 1) * pc], p.astype(BF16))
                acc = part if acc is None else acc + part
        if scores_slot is not None:
            m_scr[scores_slot] = jnp.broadcast_to(jnp.max(pm, axis=0, keepdims=True), (8, n))
        if weigh_slot is not None:
            acc_scr[weigh_slot] = acc
            l_scr[weigh_slot] = lsum

    if nq == 1:
        run((0, 0), None, None)
        run(None, 0, None)
        run(None, None, (0, 0))
        return
    run((0, 0), None, None)
    run((1, 1), 0, None)

    def pair(k, carry):
        run((0, 2 * k), 1, (0, 2 * k - 2))
        run((1, 2 * k + 1), 0, (1, 2 * k - 1))
        return carry

    lax.fori_loop(1, nq // 2, pair, 0)
    run(None, 1, (0, nq - 2))
    run(None, None, (1, nq - 1))


def _attention(aqT, k, vT, *, tq):
    B, _, T = aqT.shape
    S = k.shape[2]
    nq = T // tq
    assert nq == 1 or nq % 2 == 0
    n = GROUP * tq
    q4 = aqT.reshape(B, A_HEADS, HEAD_DIM, T)
    return pl.pallas_call(
        functools.partial(_attn_kernel, tq=tq, nq=nq, kc=256, pc=256),
        grid=(B, KV_HEADS),
        in_specs=[pl.BlockSpec((1, GROUP, HEAD_DIM, T), lambda b, kv: (b, kv, 0, 0)),
                  pl.BlockSpec((1, 1, S, HEAD_DIM), lambda b, kv: (b, kv, 0, 0)),
                  pl.BlockSpec((1, 1, HEAD_DIM, S), lambda b, kv: (b, kv, 0, 0))],
        out_specs=pl.BlockSpec((1, T, GROUP * HEAD_DIM), lambda b, kv: (b, 0, kv)),
        out_shape=jax.ShapeDtypeStruct((B, T, A_HEADS * HEAD_DIM), BF16),
        scratch_shapes=[pltpu.VMEM((2, S, n), F32), pltpu.VMEM((2, 8, n), F32),
                        pltpu.VMEM((2, HEAD_DIM, n), F32), pltpu.VMEM((2, 8, n), F32)],
        compiler_params=_params("parallel", "parallel"),
        name="block_attention",
    )(q4, k, vT)


def _merge_kernel(x_ref, mod_ref, hf_ref, hb_ref, mo_ref, ha_ref, bg_ref, mn_ref, npost_ref,
                  wbm_ref, wba_ref, wout_ref, o_ref):
    hm = hf_ref[0] + hb_ref[0]
    parts = []
    for h in range(M_HEADS):
        xh = hm[:, M_DV * h:M_DV * (h + 1)]
        ms = jnp.mean(xh * xh, axis=-1, keepdims=True)
        parts.append(xh * lax.rsqrt(ms + EPS))
    hn = jnp.concatenate(parts, axis=1) * mn_ref[...]
    hmo = (hn * _sigmoid(mo_ref[0].astype(F32))).astype(BF16)
    ym = _dot(hmo, wbm_ref[...])
    ya = _dot(ha_ref[0], wba_ref[...])
    bg = bg_ref[0].astype(F32)
    y = _sigmoid(bg[:, 0:D_MODEL]) * ym + _sigmoid(bg[:, D_MODEL:2 * D_MODEL]) * ya
    yo = _dot(y.astype(BF16), wout_ref[...])
    g1 = mod_ref[0, 5:6, :]
    o_ref[0] = x_ref[0] + g1 * _rms_rows(yo, npost_ref[...])


def _merge(x, mod, hf, hb, mo, ha, bg, mnorm, npost, wbm, wba, wout, *, tm, per_batch_mod):
    B, T, _ = x.shape
    mod_idx = (lambda b, i: (b, 0, 0)) if per_batch_mod else (lambda b, i: (0, 0, 0))
    tok = lambda n: pl.BlockSpec((1, tm, n), lambda b, i: (b, i, 0))
    return pl.pallas_call(
        _merge_kernel,
        grid=(B, T // tm),
        in_specs=[tok(D_MODEL), pl.BlockSpec((1, N_MOD, D_MODEL), mod_idx),
                  tok(1024), tok(1024), tok(1024), tok(1024), tok(2048),
                  _resident((1, 1024)), _resident((1, D_MODEL)),
                  _resident(wbm.shape), _resident(wba.shape), _resident(wout.shape)],
        out_specs=tok(D_MODEL),
        out_shape=jax.ShapeDtypeStruct(x.shape, F32),
        compiler_params=_params("parallel", "parallel"),
        name="mixer_merge",
    )(x, mod, hf, hb, mo, ha, bg, mnorm.reshape(1, 1024), npost.reshape(1, D_MODEL), wbm, wba, wout)


def _rope_tables(T):
    pairs = HEAD_DIM // 4
    t = jnp.arange(T)
    pos = jnp.stack([t // GRID_W, t % GRID_W], axis=0).astype(F32)
    inv_freq = ROPE_THETA ** (-jnp.arange(pairs, dtype=F32) / pairs)
    ang = pos[:, None, :] * inv_freq[None, :, None]
    cos, sin = jnp.cos(ang), jnp.sin(ang)
    cos_t = jnp.concatenate([cos[0], cos[0], cos[1], cos[1]], axis=0)
    sin_t = jnp.concatenate([-sin[0], sin[0], -sin[1], sin[1]], axis=0)
    return cos_t, sin_t


def _trunk(x, mod, per_batch_mod, w, ctx, *, tm):
    B, T, _ = x.shape
    L = MCHUNK
    x = _ffn(x, mod, w["npre"][0], w["npost"][0], w["wi"][0], w["wo"][0],
             j0=0, tm=tm, per_batch_mod=per_batch_mod)
    rope_tables = _rope_tables(T) if ctx is not None else None
    mq, mv, mo, bg, mkT, aqT, akT, avT, mgT = _projection(
        x, mod, w["npre"][1], w["wnat"], w["wtr"], w["gq"], w["gk"], rope_tables,
        tm=tm, per_batch_mod=per_batch_mod)

    rows, stats = _gate_scan(mgT, w["gate_bias"], L=L)
    cols = jnp.swapaxes(rows, 2, 3)
    if ctx is None:
        c0aug = jnp.zeros((B, 2, 4, 128, 256), F32)
        m0rep = jnp.zeros((B, 2, M_HEADS, 128), F32)
    else:
        c0, n0, m0 = ctx[2], ctx[3], ctx[4]
        n0rep = jnp.broadcast_to(n0[..., None], c0.shape)
        c0aug = jnp.concatenate([c0, n0rep], axis=-1).reshape(B, 2, 4, 128, 256)
        m0rep = jnp.broadcast_to(m0[..., None], (B, 2, M_HEADS, 128))
    hf, hb, c_fin, m_fin = _mlstm(mq, mkT, mv, rows, cols, stats, c0aug, m0rep, L=L)

    k_tok = jnp.swapaxes(akT.reshape(B, KV_HEADS, HEAD_DIM, T), 2, 3).astype(BF16)
    vT = avT.reshape(B, KV_HEADS, HEAD_DIM, T).astype(BF16)
    if ctx is not None:
        k_ctx = jnp.transpose(ctx[0], (0, 2, 1, 3)).astype(BF16)
        v_ctx = jnp.transpose(ctx[1], (0, 2, 3, 1)).astype(BF16)
        k_tok = jnp.concatenate([k_tok, k_ctx], axis=2)
        vT = jnp.concatenate([vT, v_ctx], axis=3)
    ha = _attention(aqT, k_tok, vT, tq=128 if ctx is not None else T)

    x = _merge(x, mod, hf, hb, mo, ha, bg, w["mnorm"], w["npost"][1], w["wbm"], w["wba"], w["wout"],
               tm=tm, per_batch_mod=per_batch_mod)
    x = _ffn(x, mod, w["npre"][2], w["npost"][2], w["wi"][1], w["wo"][1],
             j0=6, tm=tm, per_batch_mod=per_batch_mod)
    return x, (akT, avT, c_fin, m_fin)


def _layer_weights(norm_pre, norm_post, ffn_w_in, ffn_w_out, w_in_mix, gate_bias, mlstm_norm,
                   qk_norm, w_branch_m, w_branch_a, w_out):
    o = [0]
    for s in MIX_SIZES:
        o.append(o[-1] + s)
    col = lambda i: w_in_mix[:, o[i]:o[i + 1]]
    wnat = jnp.concatenate([col(0), col(2), col(3), col(8)], axis=1).astype(BF16)
    wtr = jnp.concatenate([col(1), col(5), col(6), col(7), col(4)], axis=1).T.astype(BF16)
    return dict(
        npre=norm_pre, npost=norm_post,
        wi=ffn_w_in.astype(BF16), wo=ffn_w_out.astype(BF16),
        wnat=wnat, wtr=wtr,
        gq=qk_norm[0].reshape(HEAD_DIM, 1), gk=qk_norm[1].reshape(HEAD_DIM, 1),
        gate_bias=gate_bias.reshape(32, 1), mnorm=mlstm_norm,
        wbm=w_branch_m.astype(BF16), wba=w_branch_a.astype(BF16), wout=w_out.astype(BF16))


def kernel(x_prompt, x_sample, c, cache_k, cache_v, state_C, state_n, state_m, c_ctx, w_mod, b_mod,
           norm_pre, norm_post, ffn_w_in, ffn_w_out, w_in_mix, mlstm_gate_bias, mlstm_norm, qk_norm,
           w_branch_m, w_branch_a, w_out):
    depth = w_mod.shape[0]
    nb = c.shape[0]
    y_prompt, y_sample = x_prompt, x_sample
    ks, vs, cs, ns, ms = [], [], [], [], []
    for l in range(depth):
        w = _layer_weights(norm_pre[l], norm_post[l], ffn_w_in[l], ffn_w_out[l], w_in_mix[l],
                           mlstm_gate_bias[l], mlstm_norm[l], qk_norm[l], w_branch_m[l],
                           w_branch_a[l], w_out[l])
        cvec = jnp.concatenate([c_ctx[None, :], c, jnp.zeros((16 - 1 - nb, D_MODEL), F32)], axis=0)
        mod = _modulation(cvec, w_mod[l], b_mod[l]).reshape(16, N_MOD, D_MODEL)
        y_prompt, (akT, avT, c_fin, m_fin) = _trunk(y_prompt, mod[0:1], False, w, None, tm=256)
        bp, tp = x_prompt.shape[0], x_prompt.shape[1]
        ks.append(jnp.swapaxes(akT, 1, 2).reshape(bp, tp, KV_HEADS, HEAD_DIM))
        vs.append(jnp.swapaxes(avT, 1, 2).reshape(bp, tp, KV_HEADS, HEAD_DIM))
        c_fin = c_fin.reshape(bp, 2, M_HEADS, M_DK, 2 * M_DV)
        cs.append(c_fin[..., 0:M_DV])
        ns.append(c_fin[..., M_DV])
        ms.append(m_fin[..., 0])
        ctx = (cache_k[:, l], cache_v[:, l], state_C[:, l], state_n[:, l], state_m[:, l])
        y_sample, _ = _trunk(y_sample, mod[1:1 + nb], True, w, ctx, tm=512)
    dt = x_prompt.dtype
    return (y_prompt, y_sample, jnp.stack(ks, axis=1), jnp.stack(vs, axis=1),
            jnp.stack(cs, axis=1).astype(dt), jnp.stack(ns, axis=1).astype(dt),
            jnp.stack(ms, axis=1).astype(dt))
```

```python
import functools

import jax
import jax.numpy as jnp
from jax import lax
from jax.experimental import pallas as pl
from jax.experimental.pallas import tpu as pltpu

F32 = jnp.float32
BF16 = jnp.bfloat16

D_MODEL = 1024
D_FF = 2816
M_HEADS = 8
M_DK = 64
M_DV = 128
A_HEADS = 16
KV_HEADS = 4
GROUP = A_HEADS // KV_HEADS
HEAD_DIM = 64
GRID_W = 64
ROPE_THETA = 10000.0
N_MOD = 9
EPS = 1e-6
LOG2E = 1.4426950408889634
MIX_SIZES = (512, 512, 1024, 1024, 32, 1024, 256, 256, 2048)

MCHUNK = 128
TOKEN_TILE = 512
VMEM_LIMIT_BYTES = 56 * 1024 * 1024

_NT_DIMS = (((1,), (1,)), ((), ()))


def _dot(a, b):
    return jnp.dot(a, b, preferred_element_type=F32)


def _dot_nt(a, b):
    return lax.dot_general(a, b, _NT_DIMS, preferred_element_type=F32)


def _rms_rows(x, gain_row):
    ms = jnp.mean(x * x, axis=-1, keepdims=True)
    return x * lax.rsqrt(ms + EPS) * gain_row


def _sigmoid(x):
    return jax.nn.sigmoid(x)


def _params(*sem):
    return pltpu.CompilerParams(dimension_semantics=sem, vmem_limit_bytes=VMEM_LIMIT_BYTES)


def _resident(shape):
    nd = len(shape)
    return pl.BlockSpec(shape, lambda *_: (0,) * nd, pipeline_mode=pl.Buffered(1))


def _mod_kernel(c_ref, w_ref, b_ref, o_ref):
    c = c_ref[...]
    a = c * _sigmoid(c)
    o_ref[...] = jnp.dot(a, w_ref[...], preferred_element_type=F32,
                         precision=lax.Precision.HIGHEST) + b_ref[...]


def _modulation(cvec, w_mod, b_mod):
    rows = cvec.shape[0]
    n = w_mod.shape[1]
    bn = D_MODEL
    return pl.pallas_call(
        _mod_kernel,
        grid=(n // bn,),
        in_specs=[pl.BlockSpec((rows, D_MODEL), lambda j: (0, 0)),
                  pl.BlockSpec((D_MODEL, bn), lambda j: (0, j)),
                  pl.BlockSpec((1, bn), lambda j: (0, j))],
        out_specs=pl.BlockSpec((rows, bn), lambda j: (0, j)),
        out_shape=jax.ShapeDtypeStruct((rows, n), F32),
        compiler_params=_params("arbitrary"),
        name="modulation",
    )(cvec, w_mod, b_mod.reshape(1, n))


FFN_BOUNDS = (0, 1536, D_FF)


def _ffn_kernel(x_ref, mod_ref, npre_ref, npost_ref, wi_ref, wo_ref, o_ref, *, j0, bounds):
    x = x_ref[0]
    sh = mod_ref[0, j0:j0 + 1, :]
    sc = mod_ref[0, j0 + 1:j0 + 2, :]
    g = mod_ref[0, j0 + 2:j0 + 3, :]
    h = (_rms_rows(x, npre_ref[...]) * (1.0 + sc) + sh).astype(BF16)
    acc = None
    for c0, c1 in zip(bounds[:-1], bounds[1:]):
        a = _dot(h, wi_ref[:, c0:c1])
        b = _dot(h, wi_ref[:, D_FF + c0:D_FF + c1])
        act = (a * _sigmoid(a) * b).astype(BF16)
        part = _dot(act, wo_ref[c0:c1, :])
        acc = part if acc is None else acc + part
    o_ref[0] = x + 0.5 * g * _rms_rows(acc, npost_ref[...])


def _flatten_shared(arrays, tm, per_batch_mod):
    if per_batch_mod:
        return arrays, tm
    return [a.reshape(1, -1, a.shape[-1]) for a in arrays], TOKEN_TILE


def _ffn(x, mod, npre, npost, wi, wo, *, j0, tm, per_batch_mod):
    shape = x.shape
    (x,), tm = _flatten_shared([x], tm, per_batch_mod)
    B, T, _ = x.shape
    mod_idx = (lambda b, i: (b, 0, 0)) if per_batch_mod else (lambda b, i: (0, 0, 0))
    return pl.pallas_call(
        functools.partial(_ffn_kernel, j0=j0, bounds=FFN_BOUNDS),
        grid=(B, T // tm),
        in_specs=[pl.BlockSpec((1, tm, D_MODEL), lambda b, i: (b, i, 0)),
                  pl.BlockSpec((1, N_MOD, D_MODEL), mod_idx),
                  _resident((1, D_MODEL)), _resident((1, D_MODEL)),
                  _resident(wi.shape), _resident(wo.shape)],
        out_specs=pl.BlockSpec((1, tm, D_MODEL), lambda b, i: (b, i, 0)),
        out_shape=jax.ShapeDtypeStruct(x.shape, F32),
        compiler_params=_params("parallel", "parallel"),
        name="ffn_half_step",
    )(x, mod, npre.reshape(1, D_MODEL), npost.reshape(1, D_MODEL), wi, wo).reshape(shape)


_NAT_MQ, _NAT_MV, _NAT_MO, _NAT_BG, _NAT_END = 0, 512, 1536, 2560, 4608
_TR_MK, _TR_AQ, _TR_AK, _TR_AV, _TR_MG, _TR_END = 0, 512, 1536, 1792, 2048, 2080


def _rope_swap(x):
    return jnp.concatenate([x[16:32], x[0:16], x[48:64], x[32:48]], axis=0)


def _proj_kernel(*refs, rope):
    if rope:
        (x_ref, mod_ref, npre_ref, wnat_ref, wtr_ref, gq_ref, gk_ref, cos_ref, sin_ref,
         mq_ref, mv_ref, mo_ref, bg_ref, mkT_ref, aqT_ref, akT_ref, avT_ref, mgT_ref,
         k_ref, vT_ref) = refs
    else:
        (x_ref, mod_ref, npre_ref, wnat_ref, wtr_ref, gq_ref, gk_ref,
         mq_ref, mv_ref, mo_ref, bg_ref, mkT_ref, aqT_ref, akT_ref, avT_ref, mgT_ref,
         k_ref, vT_ref) = refs
    x = x_ref[0]
    sh = mod_ref[0, 3:4, :]
    sc = mod_ref[0, 4:5, :]
    h = (_rms_rows(x, npre_ref[...]) * (1.0 + sc) + sh).astype(BF16)

    tr = _dot_nt(wtr_ref[...], h)
    mkT_ref[0] = tr[_TR_MK:_TR_AQ].astype(BF16)
    mgT_ref[0] = tr[_TR_MG:_TR_END]
    avT = tr[_TR_AV:_TR_MG]
    avT_ref[0] = avT

    def head_norm(xh, gain_col):
        ms = jnp.mean(xh * xh, axis=0, keepdims=True)
        xh = xh * lax.rsqrt(ms + EPS) * gain_col
        if rope:
            xh = xh * cos_ref[...] + _rope_swap(xh) * sin_ref[...]
        return xh

    def q_heads(lo, hi):
        for hh in range(lo, hi):
            r0 = _TR_AQ + hh * HEAD_DIM
            xh = head_norm(tr[r0:r0 + HEAD_DIM], gq_ref[...])
            aqT_ref[0, hh * HEAD_DIM:(hh + 1) * HEAD_DIM, :] = (
                xh * (LOG2E * HEAD_DIM ** -0.5)).astype(BF16)

    def kv_heads():
        for hh in range(KV_HEADS):
            r0 = hh * HEAD_DIM
            xh = head_norm(tr[_TR_AK + r0:_TR_AK + r0 + HEAD_DIM], gk_ref[...])
            akT_ref[0, r0:r0 + HEAD_DIM, :] = xh
            k_ref[0, hh] = xh.T.astype(BF16)
            vT_ref[0, hh] = avT[r0:r0 + HEAD_DIM].astype(BF16)

    mq_ref[0] = (_dot(h, wnat_ref[:, _NAT_MQ:_NAT_MV]) * (M_DK ** -0.5)).astype(BF16)
    q_heads(0, A_HEADS // 2)
    mv_ref[0] = _dot(h, wnat_ref[:, _NAT_MV:_NAT_MO]).astype(BF16)
    q_heads(A_HEADS // 2, A_HEADS)
    mo_ref[0] = _dot(h, wnat_ref[:, _NAT_MO:_NAT_BG]).astype(BF16)
    kv_heads()
    half = (_NAT_END - _NAT_BG) // 2
    bg_ref[0, :, 0:half] = _dot(h, wnat_ref[:, _NAT_BG:_NAT_BG + half]).astype(BF16)
    bg_ref[0, :, half:2 * half] = _dot(h, wnat_ref[:, _NAT_BG + half:_NAT_END]).astype(BF16)


def _projection(x, mod, npre, wnat, wtr, gq, gk, rope_tables, *, tm, per_batch_mod):
    B, T, _ = x.shape
    rope = rope_tables is not None
    mod_idx = (lambda b, i: (b, 0, 0)) if per_batch_mod else (lambda b, i: (0, 0, 0))
    tok = lambda n: pl.BlockSpec((1, tm, n), lambda b, i: (b, i, 0))
    feat = lambda n: pl.BlockSpec((1, n, tm), lambda b, i: (b, 0, i))
    in_specs = [tok(D_MODEL), pl.BlockSpec((1, N_MOD, D_MODEL), mod_idx),
                _resident((1, D_MODEL)), _resident(wnat.shape), _resident(wtr.shape),
                _resident((HEAD_DIM, 1)), _resident((HEAD_DIM, 1))]
    args = [x, mod, npre.reshape(1, D_MODEL), wnat, wtr, gq, gk]
    if rope:
        in_specs += [pl.BlockSpec((HEAD_DIM, tm), lambda b, i: (0, i))] * 2
        args += list(rope_tables)
    out_shape = [jax.ShapeDtypeStruct((B, T, 512), BF16),
                 jax.ShapeDtypeStruct((B, T, 1024), BF16),
                 jax.ShapeDtypeStruct((B, T, 1024), BF16),
                 jax.ShapeDtypeStruct((B, T, 2048), BF16),
                 jax.ShapeDtypeStruct((B, 512, T), BF16),
                 jax.ShapeDtypeStruct((B, 1024, T), BF16),
                 jax.ShapeDtypeStruct((B, 256, T), F32),
                 jax.ShapeDtypeStruct((B, 256, T), F32),
                 jax.ShapeDtypeStruct((B, 32, T), F32),
                 jax.ShapeDtypeStruct((B, KV_HEADS, T, HEAD_DIM), BF16),
                 jax.ShapeDtypeStruct((B, KV_HEADS, HEAD_DIM, T), BF16)]
    out_specs = [tok(512), tok(1024), tok(1024), tok(2048),
                 feat(512), feat(1024), feat(256), feat(256), feat(32),
                 pl.BlockSpec((1, KV_HEADS, tm, HEAD_DIM), lambda b, i: (b, 0, i, 0)),
                 pl.BlockSpec((1, KV_HEADS, HEAD_DIM, tm), lambda b, i: (b, 0, 0, i))]
    return pl.pallas_call(
        functools.partial(_proj_kernel, rope=rope),
        grid=(B, T // tm),
        in_specs=in_specs, out_specs=out_specs, out_shape=out_shape,
        compiler_params=_params("parallel", "parallel"),
        name="mixer_in_proj",
    )(*args)


def _log_sigmoid(x):
    return jnp.minimum(x, 0.0) - jnp.log1p(jnp.exp(-jnp.abs(x)))


def _gate_kernel(g_ref, bias_ref, rows_ref, stats_ref, *, L):
    T = g_ref.shape[2]
    lane = lax.broadcasted_iota(jnp.int32, (M_HEADS, T), 1) & (L - 1)
    for s in range(g_ref.shape[0]):
        g = g_ref[s] + bias_ref[...]
        for d in range(2):
            li = g[16 * d:16 * d + 8]
            lf = _log_sigmoid(g[16 * d + 8:16 * d + 16])

            def scan(v, op, fill):
                k = 1
                while k < L:
                    if d == 0:
                        sh, ok = pltpu.roll(v, k, 1), lane >= k
                    else:
                        sh, ok = pltpu.roll(v, T - k, 1), lane < L - k
                    v = op(v, jnp.where(ok, sh, fill))
                    k *= 2
                return v

            cum = scan(lf, jnp.add, 0.0)
            a = li - cum
            cm = scan(a, jnp.maximum, -jnp.inf)
            rows_ref[s, d, 0:8, :] = a
            rows_ref[s, d, 8:16, :] = cm
            rows_ref[s, d, 16:24, :] = cum
            for c in range(T // L):
                e = c * L + (L - 1 if d == 0 else 0)
                stats_ref[s, d, c, 0:8, :] = jnp.broadcast_to(cum[:, e:e + 1], (M_HEADS, 128))
                stats_ref[s, d, c, 8:16, :] = jnp.broadcast_to(cm[:, e:e + 1], (M_HEADS, 128))


def _gate_scan(mgT, bias_col, *, L):
    B, _, T = mgT.shape
    nc = T // L
    sb = max(1, min(B, 2048 // T))
    assert B % sb == 0
    return pl.pallas_call(
        functools.partial(_gate_kernel, L=L),
        grid=(B // sb,),
        in_specs=[pl.BlockSpec((sb, 32, T), lambda b: (b, 0, 0)), _resident((32, 1))],
        out_specs=[pl.BlockSpec((sb, 2, 24, T), lambda b: (b, 0, 0, 0)),
                   pl.BlockSpec((sb, 2, nc, 16, 128), lambda b: (b, 0, 0, 0, 0))],
        out_shape=[jax.ShapeDtypeStruct((B, 2, 24, T), F32),
                   jax.ShapeDtypeStruct((B, 2, nc, 16, 128), F32)],
        compiler_params=_params("parallel"),
        name="mlstm_gate_scan",
    )(mgT, bias_col)


def _mlstm_kernel(qf_ref, qb_ref, kf_ref, kb_ref, vf_ref, vb_ref, rf_ref, rb_ref, cf_ref, cb_ref,
                  sf_ref, sb_ref, c0_ref, m0_ref, hf_ref, hb_ref, cst_ref, mst_ref, *, L, group):
    j = pl.program_id(1)

    @pl.when(j == 0)
    def _():
        cst_ref[...] = c0_ref[...]
        mst_ref[...] = m0_ref[...]

    row_i = lax.broadcasted_iota(jnp.int32, (L, L), 0)
    col_i = lax.broadcasted_iota(jnp.int32, (L, L), 1)
    lane = lax.broadcasted_iota(jnp.int32, (L, 128), 1)
    ones_blk = jnp.ones((L, M_DV), BF16)
    dirs = ((qf_ref, kf_ref, vf_ref, rf_ref, cf_ref, sf_ref, hf_ref),
            (qb_ref, kb_ref, vb_ref, rb_ref, cb_ref, sb_ref, hb_ref))
    per_dir = []
    for d, refs in enumerate(dirs):
        s_ref = refs[5]
        tot = s_ref[0, 0, 0, 0:8, :]
        amax = s_ref[0, 0, 0, 8:16, :]
        m_old = mst_ref[0, d]
        m_last = jnp.maximum(m_old, amax)
        a_prev = jnp.exp(m_old - m_last)
        mst_ref[0, d] = tot + m_last
        mask = (col_i <= row_i) if d == 0 else (col_i >= row_i)
        per_dir.append((mask, m_old, m_last, a_prev, refs[4][0, 0]))

    units = [(d, h) for d in range(2) for h in range(M_HEADS)]
    for g0 in range(0, len(units), group):
        grp = units[g0:g0 + group]
        st = []
        for d, h in grp:
            q_ref, k_ref, v_ref = dirs[d][0:3]
            p, e = divmod(h, 2)
            q_pair = q_ref[0, :, 128 * p:128 * p + 128]
            keep = (lane < M_DK) if e == 0 else (lane >= M_DK)
            qm = jnp.where(keep, q_pair, jnp.zeros_like(q_pair))
            c_pair = cst_ref[0, d, p]
            st.append(dict(
                s=_dot(qm, k_ref[0, 128 * p:128 * p + 128, :]),
                inter=_dot(qm, c_pair.astype(BF16)),
                c_own=c_pair[M_DK * e:M_DK * e + M_DK],
                v_aug=jnp.concatenate([v_ref[0, :, M_DV * h:M_DV * h + M_DV], ones_blk], axis=1)))
        for u, (d, h) in zip(st, grp):
            mask, m_old, m_last, a_prev, cols = per_dir[d]
            a_row = dirs[d][3][0, 0, h:h + 1, :]
            m_h = m_old[h:h + 1, :]
            big_m = jnp.maximum(m_h, cols[:, 8 + h:9 + h])
            u["decay"] = jnp.exp(jnp.where(mask, a_row - big_m, -jnp.inf))
            u["w"] = jnp.exp(m_h - big_m)
            u["floor"] = jnp.exp(-(cols[:, 16 + h:17 + h] + big_m))
            u["kscale"] = jnp.exp(a_row - m_last[h:h + 1, :])
        for u in st:
            u["intra"] = _dot((u["s"] * u["decay"]).astype(BF16), u["v_aug"])
        for u, (d, h) in zip(st, grp):
            w = u["w"]
            num_aug = jnp.concatenate([w, w], axis=1) * u["inter"] + u["intra"]
            den = num_aug[:, M_DV:2 * M_DV]
            dirs[d][6][0, :, M_DV * h:M_DV * h + M_DV] = (
                num_aug[:, 0:M_DV] / jnp.maximum(jnp.abs(den), u["floor"]))
        for u, (d, h) in zip(st, grp):
            p, e = divmod(h, 2)
            k_h = dirs[d][1][0, M_DK * h:M_DK * h + M_DK, :]
            k_scaled = (k_h.astype(F32) * u["kscale"]).astype(BF16)
            ap = per_dir[d][3][h:h + 1, :]
            cst_ref[0, d, p, M_DK * e:M_DK * e + M_DK, :] = (
                jnp.concatenate([ap, ap], axis=1) * u["c_own"] + _dot(k_scaled, u["v_aug"]))


def _mlstm(mq, mkT, mv, rows, cols, stats, c0aug, m0rep, *, L):
    assert L == 128, "decay matrices are built as whole (L, 128) vregs"
    B, T, _ = mq.shape
    nc = T // L
    fwd = lambda b, j: j
    bwd = lambda b, j: nc - 1 - j

    def tok(n, cidx):
        return pl.BlockSpec((1, L, n), lambda b, j: (b, cidx(b, j), 0))

    def feat(n, cidx):
        return pl.BlockSpec((1, n, L), lambda b, j: (b, 0, cidx(b, j)))

    def rows_spec(d, cidx):
        return pl.BlockSpec((1, 1, 24, L), lambda b, j: (b, d, 0, cidx(b, j)))

    def cols_spec(d, cidx):
        return pl.BlockSpec((1, 1, L, 24), lambda b, j: (b, d, cidx(b, j), 0))

    def stats_spec(d, cidx):
        return pl.BlockSpec((1, 1, 1, 16, 128), lambda b, j: (b, d, cidx(b, j), 0, 0))

    state_c = pl.BlockSpec((1, 2, 4, 128, 256), lambda b, j: (b, 0, 0, 0, 0))
    state_m = pl.BlockSpec((1, 2, 8, 128), lambda b, j: (b, 0, 0, 0))
    return pl.pallas_call(
        functools.partial(_mlstm_kernel, L=L, group=4),
        grid=(B, nc),
        in_specs=[tok(512, fwd), tok(512, bwd), feat(512, fwd), feat(512, bwd),
                  tok(1024, fwd), tok(1024, bwd),
                  rows_spec(0, fwd), rows_spec(1, bwd), cols_spec(0, fwd), cols_spec(1, bwd),
                  stats_spec(0, fwd), stats_spec(1, bwd), state_c, state_m],
        out_specs=[tok(1024, fwd), tok(1024, bwd), state_c, state_m],
        out_shape=[jax.ShapeDtypeStruct((B, T, 1024), F32),
                   jax.ShapeDtypeStruct((B, T, 1024), F32),
                   jax.ShapeDtypeStruct((B, 2, 4, 128, 256), F32),
                   jax.ShapeDtypeStruct((B, 2, 8, 128), F32)],
        compiler_params=_params("parallel", "arbitrary"),
        name="mlstm_chunks",
    )(mq, mq, mkT, mkT, mv, mv, rows, rows, cols, cols, stats, stats, c0aug, m0rep)


def _attn_kernel(*refs, tq, nq, kc, pc, has_ctx):
    if has_ctx:
        q_ref, k_ref, vT_ref, kc_ref, vcT_ref, o_ref, s_scr, m_scr, acc_scr, l_scr = refs
    else:
        q_ref, k_ref, vT_ref, o_ref, s_scr, m_scr, acc_scr, l_scr = refs
        kc_ref = vcT_ref = None
    n = GROUP * tq
    T = k_ref.shape[2]
    S = s_scr.shape[1]

    def keys(r0, rows):
        if r0 < T:
            return k_ref[0, 0, r0:r0 + rows, :]
        return kc_ref[0, 0, r0 - T:r0 - T + rows, :]

    def values_t(r0, rows):
        if r0 < T:
            return vT_ref[0, 0, :, r0:r0 + rows]
        return vcT_ref[0, 0, :, r0 - T:r0 - T + rows]

    def block_start(j):
        return j * tq if isinstance(j, int) else pl.multiple_of(j * tq, tq)

    def run(scores, weigh_slot, emit):
        if emit is not None:
            emit_slot, j = emit
            l = jnp.sum(l_scr[emit_slot], axis=0, keepdims=True)
            oT = acc_scr[emit_slot] / l
            o_ref[0, pl.ds(block_start(j), tq), :] = jnp.concatenate(
                [oT[:, g * tq:(g + 1) * tq].T for g in range(GROUP)], axis=1).astype(BF16)
        scores_slot = None
        if scores is not None:
            scores_slot, j = scores
            qT = jnp.concatenate([q_ref[0, g, :, pl.ds(block_start(j), tq)]
                                  for g in range(GROUP)], axis=1)
            pm = None
        if weigh_slot is not None:
            m = m_scr[weigh_slot][0:1, :]
            acc = lsum = None
        for c in range(S // pc):
            if scores_slot is not None:
                for r0 in range(c * pc, (c + 1) * pc, kc):
                    s_c = _dot(keys(r0, kc), qT)
                    s_scr[scores_slot, r0:r0 + kc, :] = s_c
                    part = jnp.max(s_c.reshape(kc // 8, 8, n), axis=0)
                    pm = part if pm is None else jnp.maximum(pm, part)
            if weigh_slot is not None:
                p = jnp.exp2(s_scr[weigh_slot, c * pc:(c + 1) * pc, :] - m)
                psum = jnp.sum(p.reshape(pc // 8, 8, n), axis=0)
                lsum = psum if lsum is None else lsum + psum
                part = _dot(values_t(c * pc, pc), p.astype(BF16))
                acc = part if acc is None else acc + part
        if scores_slot is not None:
            m_scr[scores_slot] = jnp.broadcast_to(jnp.max(pm, axis=0, keepdims=True), (8, n))
        if weigh_slot is not None:
            acc_scr[weigh_slot] = acc
            l_scr[weigh_slot] = lsum

    if nq == 1:
        run((0, 0), None, None)
        run(None, 0, None)
        run(None, None, (0, 0))
        return
    run((0, 0), None, None)
    run((1, 1), 0, None)

    def pair(k, carry):
        run((0, 2 * k), 1, (0, 2 * k - 2))
        run((1, 2 * k + 1), 0, (1, 2 * k - 1))
        return carry

    lax.fori_loop(1, nq // 2, pair, 0)
    run(None, 1, (0, nq - 2))
    run(None, None, (1, nq - 1))


def _attention(aqT, k, vT, k_ctx, vT_ctx, *, tq):
    B, _, T = aqT.shape
    P = 0 if k_ctx is None else k_ctx.shape[2]
    S = T + P
    nq = T // tq
    assert nq == 1 or nq % 2 == 0
    n = GROUP * tq
    q4 = aqT.reshape(B, A_HEADS, HEAD_DIM, T)
    head = lambda b, kv: (b, kv, 0, 0)
    in_specs = [pl.BlockSpec((1, GROUP, HEAD_DIM, T), head),
                pl.BlockSpec((1, 1, T, HEAD_DIM), head),
                pl.BlockSpec((1, 1, HEAD_DIM, T), head)]
    args = [q4, k, vT]
    if P:
        in_specs += [pl.BlockSpec((1, 1, P, HEAD_DIM), head), pl.BlockSpec((1, 1, HEAD_DIM, P), head)]
        args += [k_ctx, vT_ctx]
    return pl.pallas_call(
        functools.partial(_attn_kernel, tq=tq, nq=nq, kc=256, pc=256, has_ctx=bool(P)),
        grid=(B, KV_HEADS),
        in_specs=in_specs,
        out_specs=pl.BlockSpec((1, T, GROUP * HEAD_DIM), lambda b, kv: (b, 0, kv)),
        out_shape=jax.ShapeDtypeStruct((B, T, A_HEADS * HEAD_DIM), BF16),
        scratch_shapes=[pltpu.VMEM((2, S, n), F32), pltpu.VMEM((2, 8, n), F32),
                        pltpu.VMEM((2, HEAD_DIM, n), F32), pltpu.VMEM((2, 8, n), F32)],
        compiler_params=_params("parallel", "parallel"),
        name="block_attention",
    )(*args)


def _merge_kernel(x_ref, mod_ref, hf_ref, hb_ref, mo_ref, ha_ref, bg_ref, mn_ref, npost_ref,
                  wbm_ref, wba_ref, wout_ref, o_ref):
    hm = hf_ref[0] + hb_ref[0]
    parts = []
    for h in range(M_HEADS):
        xh = hm[:, M_DV * h:M_DV * (h + 1)]
        ms = jnp.mean(xh * xh, axis=-1, keepdims=True)
        parts.append(xh * lax.rsqrt(ms + EPS))
    hn = jnp.concatenate(parts, axis=1) * mn_ref[...]
    hmo = (hn * _sigmoid(mo_ref[0].astype(F32))).astype(BF16)
    ym = _dot(hmo, wbm_ref[...])
    ya = _dot(ha_ref[0], wba_ref[...])
    bg = bg_ref[0].astype(F32)
    y = _sigmoid(bg[:, 0:D_MODEL]) * ym + _sigmoid(bg[:, D_MODEL:2 * D_MODEL]) * ya
    yo = _dot(y.astype(BF16), wout_ref[...])
    g1 = mod_ref[0, 5:6, :]
    o_ref[0] = x_ref[0] + g1 * _rms_rows(yo, npost_ref[...])


def _merge(x, mod, hf, hb, mo, ha, bg, mnorm, npost, wbm, wba, wout, *, tm, per_batch_mod):
    shape = x.shape
    (x, hf, hb, mo, ha, bg), tm = _flatten_shared([x, hf, hb, mo, ha, bg], tm, per_batch_mod)
    B, T, _ = x.shape
    mod_idx = (lambda b, i: (b, 0, 0)) if per_batch_mod else (lambda b, i: (0, 0, 0))
    tok = lambda n: pl.BlockSpec((1, tm, n), lambda b, i: (b, i, 0))
    return pl.pallas_call(
        _merge_kernel,
        grid=(B, T // tm),
        in_specs=[tok(D_MODEL), pl.BlockSpec((1, N_MOD, D_MODEL), mod_idx),
                  tok(1024), tok(1024), tok(1024), tok(1024), tok(2048),
                  _resident((1, 1024)), _resident((1, D_MODEL)),
                  _resident(wbm.shape), _resident(wba.shape), _resident(wout.shape)],
        out_specs=tok(D_MODEL),
        out_shape=jax.ShapeDtypeStruct(x.shape, F32),
        compiler_params=_params("parallel", "parallel"),
        name="mixer_merge",
    )(x, mod, hf, hb, mo, ha, bg, mnorm.reshape(1, 1024), npost.reshape(1, D_MODEL), wbm, wba,
      wout).reshape(shape)


def _rope_tables(T):
    pairs = HEAD_DIM // 4
    t = jnp.arange(T)
    pos = jnp.stack([t // GRID_W, t % GRID_W], axis=0).astype(F32)
    inv_freq = ROPE_THETA ** (-jnp.arange(pairs, dtype=F32) / pairs)
    ang = pos[:, None, :] * inv_freq[None, :, None]
    cos, sin = jnp.cos(ang), jnp.sin(ang)
    cos_t = jnp.concatenate([cos[0], cos[0], cos[1], cos[1]], axis=0)
    sin_t = jnp.concatenate([-sin[0], sin[0], -sin[1], sin[1]], axis=0)
    return cos_t, sin_t


def _trunk(x, mod, per_batch_mod, w, ctx, *, tm):
    B, T, _ = x.shape
    L = MCHUNK
    x = _ffn(x, mod, w["npre"][0], w["npost"][0], w["wi"][0], w["wo"][0],
             j0=0, tm=tm, per_batch_mod=per_batch_mod)
    rope_tables = _rope_tables(T) if ctx is not None else None
    mq, mv, mo, bg, mkT, aqT, akT, avT, mgT, k_tok, vT = _projection(
        x, mod, w["npre"][1], w["wnat"], w["wtr"], w["gq"], w["gk"], rope_tables,
        tm=tm, per_batch_mod=per_batch_mod)

    rows, stats = _gate_scan(mgT, w["gate_bias"], L=L)
    cols = jnp.swapaxes(rows, 2, 3)
    if ctx is None:
        c0aug = jnp.zeros((B, 2, 4, 128, 256), F32)
        m0rep = jnp.zeros((B, 2, M_HEADS, 128), F32)
    else:
        c0, n0, m0 = ctx[2], ctx[3], ctx[4]
        n0rep = jnp.broadcast_to(n0[..., None], c0.shape)
        c0aug = jnp.concatenate([c0, n0rep], axis=-1).reshape(B, 2, 4, 128, 256)
        m0rep = jnp.broadcast_to(m0[..., None], (B, 2, M_HEADS, 128))
    hf, hb, c_fin, m_fin = _mlstm(mq, mkT, mv, rows, cols, stats, c0aug, m0rep, L=L)

    if ctx is None:
        k_ctx = v_ctx = None
    else:
        k_ctx = jnp.transpose(ctx[0], (0, 2, 1, 3)).astype(BF16)
        v_ctx = jnp.transpose(ctx[1], (0, 2, 3, 1)).astype(BF16)
    ha = _attention(aqT, k_tok, vT, k_ctx, v_ctx, tq=128 if ctx is not None else T)

    x = _merge(x, mod, hf, hb, mo, ha, bg, w["mnorm"], w["npost"][1], w["wbm"], w["wba"], w["wout"],
               tm=tm, per_batch_mod=per_batch_mod)
    x = _ffn(x, mod, w["npre"][2], w["npost"][2], w["wi"][1], w["wo"][1],
             j0=6, tm=tm, per_batch_mod=per_batch_mod)
    return x, (akT, avT, c_fin, m_fin)


def _layer_weights(norm_pre, norm_post, ffn_w_in, ffn_w_out, w_in_mix, gate_bias, mlstm_norm,
                   qk_norm, w_branch_m, w_branch_a, w_out):
    o = [0]
    for s in MIX_SIZES:
        o.append(o[-1] + s)
    col = lambda i: w_in_mix[:, o[i]:o[i + 1]]
    wnat = jnp.concatenate([col(0), col(2), col(3), col(8)], axis=1).astype(BF16)
    wtr = jnp.concatenate([col(1), col(5), col(6), col(7), col(4)], axis=1).T.astype(BF16)
    return dict(
        npre=norm_pre, npost=norm_post,
        wi=ffn_w_in.astype(BF16), wo=ffn_w_out.astype(BF16),
        wnat=wnat, wtr=wtr,
        gq=qk_norm[0].reshape(HEAD_DIM, 1), gk=qk_norm[1].reshape(HEAD_DIM, 1),
        gate_bias=gate_bias.reshape(32, 1), mnorm=mlstm_norm,
        wbm=w_branch_m.astype(BF16), wba=w_branch_a.astype(BF16), wout=w_out.astype(BF16))


def kernel(x_prompt, x_sample, c, cache_k, cache_v, state_C, state_n, state_m, c_ctx, w_mod, b_mod,
           norm_pre, norm_post, ffn_w_in, ffn_w_out, w_in_mix, mlstm_gate_bias, mlstm_norm, qk_norm,
           w_branch_m, w_branch_a, w_out):
    depth = w_mod.shape[0]
    nb = c.shape[0]
    y_prompt, y_sample = x_prompt, x_sample
    ks, vs, cs, ns, ms = [], [], [], [], []
    for l in range(depth):
        w = _layer_weights(norm_pre[l], norm_post[l], ffn_w_in[l], ffn_w_out[l], w_in_mix[l],
                           mlstm_gate_bias[l], mlstm_norm[l], qk_norm[l], w_branch_m[l],
                           w_branch_a[l], w_out[l])
        cvec = jnp.concatenate([c_ctx[None, :], c, jnp.zeros((16 - 1 - nb, D_MODEL), F32)], axis=0)
        mod = _modulation(cvec, w_mod[l], b_mod[l]).reshape(16, N_MOD, D_MODEL)
        y_prompt, (akT, avT, c_fin, m_fin) = _trunk(y_prompt, mod[0:1], False, w, None, tm=256)
        bp, tp = x_prompt.shape[0], x_prompt.shape[1]
        ks.append(jnp.swapaxes(akT, 1, 2).reshape(bp, tp, KV_HEADS, HEAD_DIM))
        vs.append(jnp.swapaxes(avT, 1, 2).reshape(bp, tp, KV_HEADS, HEAD_DIM))
        c_fin = c_fin.reshape(bp, 2, M_HEADS, M_DK, 2 * M_DV)
        cs.append(c_fin[..., 0:M_DV])
        ns.append(c_fin[..., M_DV])
        ms.append(m_fin[..., 0])
        ctx = (cache_k[:, l], cache_v[:, l], state_C[:, l], state_n[:, l], state_m[:, l])
        y_sample, _ = _trunk(y_sample, mod[1:1 + nb], True, w, ctx, tm=512)
    dt = x_prompt.dtype
    return (y_prompt, y_sample, jnp.stack(ks, axis=1), jnp.stack(vs, axis=1),
            jnp.stack(cs, axis=1).astype(dt), jnp.stack(ns, axis=1).astype(dt),
            jnp.stack(ms, axis=1).astype(dt))
```

```python
import functools

import jax
import jax.numpy as jnp
from jax import lax
from jax.experimental import pallas as pl
from jax.experimental.pallas import tpu as pltpu

F32 = jnp.float32
BF16 = jnp.bfloat16

D_MODEL = 1024
D_FF = 2816
M_HEADS = 8
M_DK = 64
M_DV = 128
A_HEADS = 16
KV_HEADS = 4
GROUP = A_HEADS // KV_HEADS
HEAD_DIM = 64
GRID_W = 64
ROPE_THETA = 10000.0
N_MOD = 9
EPS = 1e-6
LOG2E = 1.4426950408889634
LN2 = 0.6931471805599453
MIX_SIZES = (512, 512, 1024, 1024, 32, 1024, 256, 256, 2048)

MCHUNK = 128
TOKEN_TILE = 512
VMEM_LIMIT_BYTES = 56 * 1024 * 1024

_NT_DIMS = (((1,), (1,)), ((), ()))


def _dot(a, b):
    return jnp.dot(a, b, preferred_element_type=F32)


def _dot_nt(a, b):
    return lax.dot_general(a, b, _NT_DIMS, preferred_element_type=F32)


def _rms_rows(x, gain_row):
    ms = jnp.mean(x * x, axis=-1, keepdims=True)
    return x * lax.rsqrt(ms + EPS) * gain_row


def _sigmoid(x):
    return jax.nn.sigmoid(x)


def _params(*sem):
    return pltpu.CompilerParams(dimension_semantics=sem, vmem_limit_bytes=VMEM_LIMIT_BYTES)


def _resident(shape):
    nd = len(shape)
    return pl.BlockSpec(shape, lambda *_: (0,) * nd, pipeline_mode=pl.Buffered(1))


def _mod_kernel(c_ref, w_ref, b_ref, o_ref):
    c = c_ref[...]
    a = c * _sigmoid(c)
    o_ref[...] = jnp.dot(a, w_ref[...], preferred_element_type=F32,
                         precision=lax.Precision.HIGHEST) + b_ref[...]


def _modulation(cvec, w_mod, b_mod):
    rows = cvec.shape[0]
    n = w_mod.shape[1]
    bn = D_MODEL
    return pl.pallas_call(
        _mod_kernel,
        grid=(n // bn,),
        in_specs=[pl.BlockSpec((rows, D_MODEL), lambda j: (0, 0)),
                  pl.BlockSpec((D_MODEL, bn), lambda j: (0, j)),
                  pl.BlockSpec((1, bn), lambda j: (0, j))],
        out_specs=pl.BlockSpec((rows, bn), lambda j: (0, j)),
        out_shape=jax.ShapeDtypeStruct((rows, n), F32),
        compiler_params=_params("arbitrary"),
        name="modulation",
    )(cvec, w_mod, b_mod.reshape(1, n))


FFN_BOUNDS = (0, 1536, D_FF)


def _ffn_kernel(x_ref, mod_ref, npre_ref, npost_ref, wi_ref, wo_ref, o_ref, *, j0, bounds):
    x = x_ref[0]
    sh = mod_ref[0, j0:j0 + 1, :]
    sc = mod_ref[0, j0 + 1:j0 + 2, :]
    g = mod_ref[0, j0 + 2:j0 + 3, :]
    h = (_rms_rows(x, npre_ref[...]) * (1.0 + sc) + sh).astype(BF16)
    acc = None
    for c0, c1 in zip(bounds[:-1], bounds[1:]):
        a = _dot(h, wi_ref[:, c0:c1])
        b = _dot(h, wi_ref[:, D_FF + c0:D_FF + c1])
        act = (a * _sigmoid(a) * b).astype(BF16)
        part = _dot(act, wo_ref[c0:c1, :])
        acc = part if acc is None else acc + part
    o_ref[0] = x + 0.5 * g * _rms_rows(acc, npost_ref[...])


def _flatten_shared(arrays, tm, per_batch_mod):
    if per_batch_mod:
        return arrays, tm
    return [a.reshape(1, -1, a.shape[-1]) for a in arrays], TOKEN_TILE


def _ffn(x, mod, npre, npost, wi, wo, which, *, j0, tm, per_batch_mod):
    shape = x.shape
    (x,), tm = _flatten_shared([x], tm, per_batch_mod)
    B, T, _ = x.shape
    mod_idx = (lambda b, i: (b, 0, 0)) if per_batch_mod else (lambda b, i: (0, 0, 0))
    weight = lambda a: pl.BlockSpec((None,) + a.shape[1:], lambda b, i: (which, 0, 0),
                                    pipeline_mode=pl.Buffered(1))
    return pl.pallas_call(
        functools.partial(_ffn_kernel, j0=j0, bounds=FFN_BOUNDS),
        grid=(B, T // tm),
        in_specs=[pl.BlockSpec((1, tm, D_MODEL), lambda b, i: (b, i, 0)),
                  pl.BlockSpec((1, N_MOD, D_MODEL), mod_idx),
                  _resident((1, D_MODEL)), _resident((1, D_MODEL)),
                  weight(wi), weight(wo)],
        out_specs=pl.BlockSpec((1, tm, D_MODEL), lambda b, i: (b, i, 0)),
        out_shape=jax.ShapeDtypeStruct(x.shape, F32),
        compiler_params=_params("parallel", "parallel"),
        name="ffn_half_step",
    )(x, mod, npre.reshape(1, D_MODEL), npost.reshape(1, D_MODEL), wi, wo).reshape(shape)


_NAT_MQ, _NAT_MV, _NAT_MO, _NAT_BG, _NAT_END = 0, 512, 1536, 2560, 4608
_TR_MK, _TR_AQ, _TR_AK, _TR_AV, _TR_MG, _TR_END = 0, 512, 1536, 1792, 2048, 2080


def _rope_swap(x):
    return jnp.concatenate([x[16:32], x[0:16], x[48:64], x[32:48]], axis=0)


def _proj_kernel(*refs, rope):
    if rope:
        (x_ref, mod_ref, npre_ref, wnat_ref, wtr_ref, gq_ref, gk_ref, cos_ref, sin_ref,
         mq_ref, mv_ref, mo_ref, bg_ref, mkT_ref, aqT_ref, kn_ref, vn_ref, mgT_ref,
         k_ref, vT_ref) = refs
    else:
        (x_ref, mod_ref, npre_ref, wnat_ref, wtr_ref, gq_ref, gk_ref,
         mq_ref, mv_ref, mo_ref, bg_ref, mkT_ref, aqT_ref, kn_ref, vn_ref, mgT_ref,
         k_ref, vT_ref) = refs
    x = x_ref[0]
    sh = mod_ref[0, 3:4, :]
    sc = mod_ref[0, 4:5, :]
    h = (_rms_rows(x, npre_ref[...]) * (1.0 + sc) + sh).astype(BF16)

    tr = _dot_nt(wtr_ref[...], h)
    mkT_ref[0] = tr[_TR_MK:_TR_AQ].astype(BF16)
    mgT_ref[0] = tr[_TR_MG:_TR_END]
    avT = tr[_TR_AV:_TR_MG]
    vn_ref[0] = avT.T

    def head_norm(xh, gain_col):
        ms = jnp.mean(xh * xh, axis=0, keepdims=True)
        xh = xh * lax.rsqrt(ms + EPS) * gain_col
        if rope:
            xh = xh * cos_ref[...] + _rope_swap(xh) * sin_ref[...]
        return xh

    def q_heads(lo, hi):
        for hh in range(lo, hi):
            r0 = _TR_AQ + hh * HEAD_DIM
            xh = head_norm(tr[r0:r0 + HEAD_DIM], gq_ref[...])
            aqT_ref[0, hh * HEAD_DIM:(hh + 1) * HEAD_DIM, :] = (
                xh * (LOG2E * HEAD_DIM ** -0.5)).astype(BF16)

    def kv_heads():
        normed = []
        for hh in range(KV_HEADS):
            r0 = hh * HEAD_DIM
            xh = head_norm(tr[_TR_AK + r0:_TR_AK + r0 + HEAD_DIM], gk_ref[...])
            normed.append(xh)
            k_ref[0, hh] = xh.T.astype(BF16)
            vT_ref[0, hh] = avT[r0:r0 + HEAD_DIM].astype(BF16)
        kn_ref[0] = jnp.concatenate(normed, axis=0).T

    mq_ref[0] = (_dot(h, wnat_ref[:, _NAT_MQ:_NAT_MV]) * (M_DK ** -0.5)).astype(BF16)
    q_heads(0, A_HEADS // 2)
    mv_ref[0] = _dot(h, wnat_ref[:, _NAT_MV:_NAT_MO]).astype(BF16)
    q_heads(A_HEADS // 2, A_HEADS)
    mo_ref[0] = _dot(h, wnat_ref[:, _NAT_MO:_NAT_BG]).astype(BF16)
    kv_heads()
    half = (_NAT_END - _NAT_BG) // 2
    bg_ref[0, :, 0:half] = _dot(h, wnat_ref[:, _NAT_BG:_NAT_BG + half]).astype(BF16)
    bg_ref[0, :, half:2 * half] = _dot(h, wnat_ref[:, _NAT_BG + half:_NAT_END]).astype(BF16)


def _projection(x, mod, npre, wnat, wtr, gq, gk, rope_tables, *, tm, per_batch_mod):
    B, T, _ = x.shape
    rope = rope_tables is not None
    mod_idx = (lambda b, i: (b, 0, 0)) if per_batch_mod else (lambda b, i: (0, 0, 0))
    tok = lambda n: pl.BlockSpec((1, tm, n), lambda b, i: (b, i, 0))
    feat = lambda n: pl.BlockSpec((1, n, tm), lambda b, i: (b, 0, i))
    in_specs = [tok(D_MODEL), pl.BlockSpec((1, N_MOD, D_MODEL), mod_idx),
                _resident((1, D_MODEL)), _resident(wnat.shape), _resident(wtr.shape),
                _resident((HEAD_DIM, 1)), _resident((HEAD_DIM, 1))]
    args = [x, mod, npre.reshape(1, D_MODEL), wnat, wtr, gq, gk]
    if rope:
        in_specs += [pl.BlockSpec((HEAD_DIM, tm), lambda b, i: (0, i))] * 2
        args += list(rope_tables)
    out_shape = [jax.ShapeDtypeStruct((B, T, 512), BF16),
                 jax.ShapeDtypeStruct((B, T, 1024), BF16),
                 jax.ShapeDtypeStruct((B, T, 1024), BF16),
                 jax.ShapeDtypeStruct((B, T, 2048), BF16),
                 jax.ShapeDtypeStruct((B, 512, T), BF16),
                 jax.ShapeDtypeStruct((B, 1024, T), BF16),
                 jax.ShapeDtypeStruct((B, T, 256), F32),
                 jax.ShapeDtypeStruct((B, T, 256), F32),
                 jax.ShapeDtypeStruct((B, 32, T), F32),
                 jax.ShapeDtypeStruct((B, KV_HEADS, T, HEAD_DIM), BF16),
                 jax.ShapeDtypeStruct((B, KV_HEADS, HEAD_DIM, T), BF16)]
    out_specs = [tok(512), tok(1024), tok(1024), tok(2048),
                 feat(512), feat(1024), tok(256), tok(256), feat(32),
                 pl.BlockSpec((1, KV_HEADS, tm, HEAD_DIM), lambda b, i: (b, 0, i, 0)),
                 pl.BlockSpec((1, KV_HEADS, HEAD_DIM, tm), lambda b, i: (b, 0, 0, i))]
    return pl.pallas_call(
        functools.partial(_proj_kernel, rope=rope),
        grid=(B, T // tm),
        in_specs=in_specs, out_specs=out_specs, out_shape=out_shape,
        compiler_params=_params("parallel", "parallel"),
        name="mixer_in_proj",
    )(*args)


def _log_sigmoid(x):
    return jnp.minimum(x, 0.0) - jnp.log1p(jnp.exp(-jnp.abs(x)))


def _gate_kernel(g_ref, bias_ref, rows_ref, stats_ref, *, L):
    T = g_ref.shape[2]
    lane = lax.broadcasted_iota(jnp.int32, (M_HEADS, T), 1) & (L - 1)
    for s in range(g_ref.shape[0]):
        g = g_ref[s] + bias_ref[...]
        for d in range(2):
            li = g[16 * d:16 * d + 8] * LOG2E
            lf = _log_sigmoid(g[16 * d + 8:16 * d + 16]) * LOG2E

            def scan(v, op, fill):
                k = 1
                while k < L:
                    if d == 0:
                        sh, ok = pltpu.roll(v, k, 1), lane >= k
                    else:
                        sh, ok = pltpu.roll(v, T - k, 1), lane < L - k
                    v = op(v, jnp.where(ok, sh, fill))
                    k *= 2
                return v

            cum = scan(lf, jnp.add, 0.0)
            a = li - cum
            cm = scan(a, jnp.maximum, -jnp.inf)
            rows_ref[s, d, 0:8, :] = a
            rows_ref[s, d, 8:16, :] = cm
            rows_ref[s, d, 16:24, :] = cum
            for c in range(T // L):
                e = c * L + (L - 1 if d == 0 else 0)
                stats_ref[s, d, c, 0:8, :] = jnp.broadcast_to(cum[:, e:e + 1], (M_HEADS, 128))
                stats_ref[s, d, c, 8:16, :] = jnp.broadcast_to(cm[:, e:e + 1], (M_HEADS, 128))


def _gate_scan(mgT, bias_col, *, L):
    B, _, T = mgT.shape
    nc = T // L
    sb = max(1, min(B, 2048 // T))
    assert B % sb == 0
    return pl.pallas_call(
        functools.partial(_gate_kernel, L=L),
        grid=(B // sb,),
        in_specs=[pl.BlockSpec((sb, 32, T), lambda b: (b, 0, 0)), _resident((32, 1))],
        out_specs=[pl.BlockSpec((sb, 2, 24, T), lambda b: (b, 0, 0, 0)),
                   pl.BlockSpec((sb, 2, nc, 16, 128), lambda b: (b, 0, 0, 0, 0))],
        out_shape=[jax.ShapeDtypeStruct((B, 2, 24, T), F32),
                   jax.ShapeDtypeStruct((B, 2, nc, 16, 128), F32)],
        compiler_params=_params("parallel"),
        name="mlstm_gate_scan",
    )(mgT, bias_col)


def _mlstm_kernel(qf_ref, qb_ref, kf_ref, kb_ref, vf_ref, vb_ref, rf_ref, rb_ref, cf_ref, cb_ref,
                  sf_ref, sb_ref, c0_ref, m0_ref, hf_ref, hb_ref, cst_ref, mst_ref, *, L, group):
    j = pl.program_id(1)

    @pl.when(j == 0)
    def _():
        cst_ref[...] = c0_ref[...]
        mst_ref[...] = m0_ref[...]

    row_i = lax.broadcasted_iota(jnp.int32, (L, L), 0)
    col_i = lax.broadcasted_iota(jnp.int32, (L, L), 1)
    lane = lax.broadcasted_iota(jnp.int32, (L, 128), 1)
    ones_blk = jnp.ones((L, M_DV), BF16)
    dirs = ((qf_ref, kf_ref, vf_ref, rf_ref, cf_ref, sf_ref, hf_ref),
            (qb_ref, kb_ref, vb_ref, rb_ref, cb_ref, sb_ref, hb_ref))
    per_dir = []
    for d, refs in enumerate(dirs):
        s_ref = refs[5]
        tot = s_ref[0, 0, 0, 0:8, :]
        amax = s_ref[0, 0, 0, 8:16, :]
        m_old = mst_ref[0, d]
        m_last = jnp.maximum(m_old, amax)
        a_prev = jnp.exp2(m_old - m_last)
        mst_ref[0, d] = tot + m_last
        mask = (col_i <= row_i) if d == 0 else (col_i >= row_i)
        per_dir.append((mask, m_old, m_last, a_prev, refs[4][0, 0]))

    units = [(d, h) for d in range(2) for h in range(M_HEADS)]
    for g0 in range(0, len(units), group):
        grp = units[g0:g0 + group]
        st = []
        pair_state = {}
        for d, h in grp:
            q_ref, k_ref, v_ref = dirs[d][0:3]
            p, e = divmod(h, 2)
            q_pair = q_ref[0, :, 128 * p:128 * p + 128]
            keep = (lane < M_DK) if e == 0 else (lane >= M_DK)
            qm = jnp.where(keep, q_pair, jnp.zeros_like(q_pair))
            if (d, p) not in pair_state:
                c_pair = cst_ref[0, d, p]
                pair_state[d, p] = (c_pair, c_pair.astype(BF16))
            c_pair, c_pair_bf = pair_state[d, p]
            st.append(dict(
                qm=qm, c_pair_bf=c_pair_bf,
                s=_dot(qm, k_ref[0, 128 * p:128 * p + 128, :]),
                c_own=c_pair[M_DK * e:M_DK * e + M_DK],
                v_aug=jnp.concatenate([v_ref[0, :, M_DV * h:M_DV * h + M_DV], ones_blk], axis=1)))
        for u, (d, h) in zip(st, grp):
            mask, m_old, m_last, a_prev, cols = per_dir[d]
            a_row = dirs[d][3][0, 0, h:h + 1, :]
            m_h = m_old[h:h + 1, :]
            big_m = jnp.maximum(m_h, cols[:, 8 + h:9 + h])
            u["decay"] = jnp.exp2(jnp.where(mask, a_row - big_m, -jnp.inf))
            u["w"] = jnp.exp2(m_h - big_m)
            u["floor"] = jnp.exp2(-(cols[:, 16 + h:17 + h] + big_m))
            u["kscale"] = jnp.exp2(a_row - m_last[h:h + 1, :])
        for u in st:
            lhs = jnp.concatenate([(u["s"] * u["decay"]).astype(BF16),
                                   u["w"].astype(BF16) * u["qm"]], axis=1)
            u["num_aug"] = _dot(lhs, jnp.concatenate([u["v_aug"], u["c_pair_bf"]], axis=0))
        for u, (d, h) in zip(st, grp):
            num_aug = u["num_aug"]
            den = num_aug[:, M_DV:2 * M_DV]
            dirs[d][6][0, :, M_DV * h:M_DV * h + M_DV] = (
                num_aug[:, 0:M_DV] / jnp.maximum(jnp.abs(den), u["floor"]))
        for u, (d, h) in zip(st, grp):
            p, e = divmod(h, 2)
            k_h = dirs[d][1][0, M_DK * h:M_DK * h + M_DK, :]
            k_scaled = (k_h.astype(F32) * u["kscale"]).astype(BF16)
            ap = per_dir[d][3][h:h + 1, :]
            cst_ref[0, d, p, M_DK * e:M_DK * e + M_DK, :] = (
                jnp.concatenate([ap, ap], axis=1) * u["c_own"] + _dot(k_scaled, u["v_aug"]))


def _mlstm(mq, mkT, mv, rows, cols, stats, c0aug, m0rep, *, L):
    assert L == 128, "decay matrices are built as whole (L, 128) vregs"
    B, T, _ = mq.shape
    nc = T // L
    fwd = lambda b, j: j
    bwd = lambda b, j: nc - 1 - j

    def tok(n, cidx):
        return pl.BlockSpec((1, L, n), lambda b, j: (b, cidx(b, j), 0))

    def feat(n, cidx):
        return pl.BlockSpec((1, n, L), lambda b, j: (b, 0, cidx(b, j)))

    def rows_spec(d, cidx):
        return pl.BlockSpec((1, 1, 24, L), lambda b, j: (b, d, 0, cidx(b, j)))

    def cols_spec(d, cidx):
        return pl.BlockSpec((1, 1, L, 24), lambda b, j: (b, d, cidx(b, j), 0))

    def stats_spec(d, cidx):
        return pl.BlockSpec((1, 1, 1, 16, 128), lambda b, j: (b, d, cidx(b, j), 0, 0))

    state_c = pl.BlockSpec((1, 2, 4, 128, 256), lambda b, j: (b, 0, 0, 0, 0))
    state_m = pl.BlockSpec((1, 2, 8, 128), lambda b, j: (b, 0, 0, 0))
    return pl.pallas_call(
        functools.partial(_mlstm_kernel, L=L, group=4),
        grid=(B, nc),
        in_specs=[tok(512, fwd), tok(512, bwd), feat(512, fwd), feat(512, bwd),
                  tok(1024, fwd), tok(1024, bwd),
                  rows_spec(0, fwd), rows_spec(1, bwd), cols_spec(0, fwd), cols_spec(1, bwd),
                  stats_spec(0, fwd), stats_spec(1, bwd), state_c, state_m],
        out_specs=[tok(1024, fwd), tok(1024, bwd), state_c, state_m],
        out_shape=[jax.ShapeDtypeStruct((B, T, 1024), F32),
                   jax.ShapeDtypeStruct((B, T, 1024), F32),
                   jax.ShapeDtypeStruct((B, 2, 4, 128, 256), F32),
                   jax.ShapeDtypeStruct((B, 2, 8, 128), F32)],
        compiler_params=_params("parallel", "arbitrary"),
        name="mlstm_chunks",
    )(mq, mq, mkT, mkT, mv, mv, rows, rows, cols, cols, stats, stats, c0aug, m0rep)


def _attn_kernel(*refs, tq, nq, kc, pc, has_ctx):
    if has_ctx:
        q_ref, k_ref, vT_ref, kc_ref, vcT_ref, o_ref, s_scr, m_scr, acc_scr, l_scr = refs
    else:
        q_ref, k_ref, vT_ref, o_ref, s_scr, m_scr, acc_scr, l_scr = refs
        kc_ref = vcT_ref = None
    n = GROUP * tq
    T = k_ref.shape[2]
    S = s_scr.shape[1]

    def keys(r0, rows):
        if r0 < T:
            return k_ref[0, 0, r0:r0 + rows, :]
        return kc_ref[0, 0, r0 - T:r0 - T + rows, :]

    def values_t(r0, rows):
        if r0 < T:
            return vT_ref[0, 0, :, r0:r0 + rows]
        return vcT_ref[0, 0, :, r0 - T:r0 - T + rows]

    def block_start(j):
        return j * tq if isinstance(j, int) else pl.multiple_of(j * tq, tq)

    def run(scores, weigh_slot, emit):
        if emit is not None:
            emit_slot, j = emit
            acc_all = acc_scr[emit_slot]
            oT = acc_all[0:HEAD_DIM] / acc_all[HEAD_DIM:HEAD_DIM + 1]
            o_ref[0, pl.ds(block_start(j), tq), :] = jnp.concatenate(
                [oT[:, g * tq:(g + 1) * tq].T for g in range(GROUP)], axis=1).astype(BF16)
        scores_slot = None
        if scores is not None:
            scores_slot, j = scores
            qT = jnp.concatenate([q_ref[0, g, :, pl.ds(block_start(j), tq)]
                                  for g in range(GROUP)], axis=1)
            pm = None
        if weigh_slot is not None:
            m = m_scr[weigh_slot][0:1, :]
            acc = None
            ones_rows = jnp.ones((16, pc), BF16)
        for c in range(S // pc):
            if scores_slot is not None:
                for r0 in range(c * pc, (c + 1) * pc, kc):
                    s_c = _dot(keys(r0, kc), qT)
                    s_scr[scores_slot, r0:r0 + kc, :] = s_c
                    part = jnp.max(s_c.reshape(kc // 8, 8, n), axis=0)
                    pm = part if pm is None else jnp.maximum(pm, part)
            if weigh_slot is not None:
                p = jnp.exp2(s_scr[weigh_slot, c * pc:(c + 1) * pc, :] - m)
                v_aug = jnp.concatenate([values_t(c * pc, pc), ones_rows], axis=0)
                part = _dot(v_aug, p.astype(BF16))
                acc = part if acc is None else acc + part
        if scores_slot is not None:
            m_scr[scores_slot] = jnp.broadcast_to(jnp.max(pm, axis=0, keepdims=True), (8, n))
        if weigh_slot is not None:
            acc_scr[weigh_slot] = acc

    if nq == 1:
        run((0, 0), None, None)
        run(None, 0, None)
        run(None, None, (0, 0))
        return
    run((0, 0), None, None)
    run((1, 1), 0, None)

    def pair(k, carry):
        run((0, 2 * k), 1, (0, 2 * k - 2))
        run((1, 2 * k + 1), 0, (1, 2 * k - 1))
        return carry

    lax.fori_loop(1, nq // 2, pair, 0)
    run(None, 1, (0, nq - 2))
    run(None, None, (1, nq - 1))


def _attention(aqT, k, vT, k_ctx, vT_ctx, *, tq):
    B, _, T = aqT.shape
    P = 0 if k_ctx is None else k_ctx.shape[2]
    S = T + P
    nq = T // tq
    assert nq == 1 or nq % 2 == 0
    n = GROUP * tq
    q4 = aqT.reshape(B, A_HEADS, HEAD_DIM, T)
    head = lambda b, kv: (b, kv, 0, 0)
    in_specs = [pl.BlockSpec((1, GROUP, HEAD_DIM, T), head),
                pl.BlockSpec((1, 1, T, HEAD_DIM), head),
                pl.BlockSpec((1, 1, HEAD_DIM, T), head)]
    args = [q4, k, vT]
    if P:
        in_specs += [pl.BlockSpec((1, 1, P, HEAD_DIM), head), pl.BlockSpec((1, 1, HEAD_DIM, P), head)]
        args += [k_ctx, vT_ctx]
    return pl.pallas_call(
        functools.partial(_attn_kernel, tq=tq, nq=nq, kc=256, pc=256, has_ctx=bool(P)),
        grid=(B, KV_HEADS),
        in_specs=in_specs,
        out_specs=pl.BlockSpec((1, T, GROUP * HEAD_DIM), lambda b, kv: (b, 0, kv)),
        out_shape=jax.ShapeDtypeStruct((B, T, A_HEADS * HEAD_DIM), BF16),
        scratch_shapes=[pltpu.VMEM((2, S, n), F32), pltpu.VMEM((2, 8, n), F32),
                        pltpu.VMEM((2, HEAD_DIM + 16, n), F32), pltpu.VMEM((2, 8, n), F32)],
        compiler_params=_params("parallel", "parallel"),
        name="block_attention",
    )(*args)


def _merge_kernel(x_ref, mod_ref, hf_ref, hb_ref, mo_ref, ha_ref, bg_ref, mn_ref, npost_ref,
                  wbm_ref, wba_ref, wout_ref, o_ref):
    hm = hf_ref[0] + hb_ref[0]
    parts = []
    for h in range(M_HEADS):
        xh = hm[:, M_DV * h:M_DV * (h + 1)]
        ms = jnp.mean(xh * xh, axis=-1, keepdims=True)
        parts.append(xh * lax.rsqrt(ms + EPS))
    hn = jnp.concatenate(parts, axis=1) * mn_ref[...]
    hmo = (hn * _sigmoid(mo_ref[0].astype(F32))).astype(BF16)
    ym = _dot(hmo, wbm_ref[...])
    ya = _dot(ha_ref[0], wba_ref[...])
    bg = bg_ref[0].astype(F32)
    y = _sigmoid(bg[:, 0:D_MODEL]) * ym + _sigmoid(bg[:, D_MODEL:2 * D_MODEL]) * ya
    yo = _dot(y.astype(BF16), wout_ref[...])
    g1 = mod_ref[0, 5:6, :]
    o_ref[0] = x_ref[0] + g1 * _rms_rows(yo, npost_ref[...])


def _merge(x, mod, hf, hb, mo, ha, bg, mnorm, npost, wbm, wba, wout, *, tm, per_batch_mod):
    shape = x.shape
    (x, hf, hb, mo, ha, bg), tm = _flatten_shared([x, hf, hb, mo, ha, bg], tm, per_batch_mod)
    B, T, _ = x.shape
    mod_idx = (lambda b, i: (b, 0, 0)) if per_batch_mod else (lambda b, i: (0, 0, 0))
    tok = lambda n: pl.BlockSpec((1, tm, n), lambda b, i: (b, i, 0))
    return pl.pallas_call(
        _merge_kernel,
        grid=(B, T // tm),
        in_specs=[tok(D_MODEL), pl.BlockSpec((1, N_MOD, D_MODEL), mod_idx),
                  tok(1024), tok(1024), tok(1024), tok(1024), tok(2048),
                  _resident((1, 1024)), _resident((1, D_MODEL)),
                  _resident(wbm.shape), _resident(wba.shape), _resident(wout.shape)],
        out_specs=tok(D_MODEL),
        out_shape=jax.ShapeDtypeStruct(x.shape, F32),
        compiler_params=_params("parallel", "parallel"),
        name="mixer_merge",
    )(x, mod, hf, hb, mo, ha, bg, mnorm.reshape(1, 1024), npost.reshape(1, D_MODEL), wbm, wba,
      wout).reshape(shape)


def _rope_tables(T):
    pairs = HEAD_DIM // 4
    t = jnp.arange(T)
    pos = jnp.stack([t // GRID_W, t % GRID_W], axis=0).astype(F32)
    inv_freq = ROPE_THETA ** (-jnp.arange(pairs, dtype=F32) / pairs)
    ang = pos[:, None, :] * inv_freq[None, :, None]
    cos, sin = jnp.cos(ang), jnp.sin(ang)
    cos_t = jnp.concatenate([cos[0], cos[0], cos[1], cos[1]], axis=0)
    sin_t = jnp.concatenate([-sin[0], sin[0], -sin[1], sin[1]], axis=0)
    return cos_t, sin_t


def _trunk(x, mod, per_batch_mod, w, ctx, *, tm):
    B, T, _ = x.shape
    L = MCHUNK
    x = _ffn(x, mod, w["npre"][0], w["npost"][0], w["wi"], w["wo"], 0,
             j0=0, tm=tm, per_batch_mod=per_batch_mod)
    rope_tables = _rope_tables(T) if ctx is not None else None
    mq, mv, mo, bg, mkT, aqT, k_nat, v_nat, mgT, k_tok, vT = _projection(
        x, mod, w["npre"][1], w["wnat"], w["wtr"], w["gq"], w["gk"], rope_tables,
        tm=tm, per_batch_mod=per_batch_mod)

    rows, stats = _gate_scan(mgT, w["gate_bias"], L=L)
    cols = jnp.swapaxes(rows, 2, 3)
    if ctx is None:
        c0aug = jnp.zeros((B, 2, 4, 128, 256), F32)
        m0rep = jnp.zeros((B, 2, M_HEADS, 128), F32)
    else:
        c0, n0, m0 = ctx[2], ctx[3], ctx[4]
        n0rep = jnp.broadcast_to(n0[..., None], c0.shape)
        c0aug = jnp.concatenate([c0, n0rep], axis=-1).reshape(B, 2, 4, 128, 256)
        m0rep = jnp.broadcast_to(m0[..., None] * LOG2E, (B, 2, M_HEADS, 128))
    hf, hb, c_fin, m_fin = _mlstm(mq, mkT, mv, rows, cols, stats, c0aug, m0rep, L=L)
    m_fin = m_fin[..., 0] * LN2

    if ctx is None:
        k_ctx = v_ctx = None
    else:
        k_ctx = jnp.transpose(ctx[0], (0, 2, 1, 3)).astype(BF16)
        v_ctx = jnp.transpose(ctx[1], (0, 2, 3, 1)).astype(BF16)
    ha = _attention(aqT, k_tok, vT, k_ctx, v_ctx, tq=128 if ctx is not None else T)

    x = _merge(x, mod, hf, hb, mo, ha, bg, w["mnorm"], w["npost"][1], w["wbm"], w["wba"], w["wout"],
               tm=tm, per_batch_mod=per_batch_mod)
    x = _ffn(x, mod, w["npre"][2], w["npost"][2], w["wi"], w["wo"], 1,
             j0=6, tm=tm, per_batch_mod=per_batch_mod)
    return x, (k_nat, v_nat, c_fin, m_fin)


def _layer_weights(norm_pre, norm_post, ffn_w_in, ffn_w_out, w_in_mix, gate_bias, mlstm_norm,
                   qk_norm, w_branch_m, w_branch_a, w_out):
    o = [0]
    for s in MIX_SIZES:
        o.append(o[-1] + s)
    col = lambda i: w_in_mix[:, o[i]:o[i + 1]]
    wnat = jnp.concatenate([col(0), col(2), col(3), col(8)], axis=1).astype(BF16)
    wtr = jnp.concatenate([col(1), col(5), col(6), col(7), col(4)], axis=1).T.astype(BF16)
    return dict(
        npre=norm_pre, npost=norm_post,
        wi=ffn_w_in.astype(BF16), wo=ffn_w_out.astype(BF16),
        wnat=wnat, wtr=wtr,
        gq=qk_norm[0].reshape(HEAD_DIM, 1), gk=qk_norm[1].reshape(HEAD_DIM, 1),
        gate_bias=gate_bias.reshape(32, 1), mnorm=mlstm_norm,
        wbm=w_branch_m.astype(BF16), wba=w_branch_a.astype(BF16), wout=w_out.astype(BF16))


def kernel(x_prompt, x_sample, c, cache_k, cache_v, state_C, state_n, state_m, c_ctx, w_mod, b_mod,
           norm_pre, norm_post, ffn_w_in, ffn_w_out, w_in_mix, mlstm_gate_bias, mlstm_norm, qk_norm,
           w_branch_m, w_branch_a, w_out):
    depth = w_mod.shape[0]
    nb = c.shape[0]
    y_prompt, y_sample = x_prompt, x_sample
    ks, vs, cs, ns, ms = [], [], [], [], []
    for l in range(depth):
        w = _layer_weights(norm_pre[l], norm_post[l], ffn_w_in[l], ffn_w_out[l], w_in_mix[l],
                           mlstm_gate_bias[l], mlstm_norm[l], qk_norm[l], w_branch_m[l],
                           w_branch_a[l], w_out[l])
        cvec = jnp.concatenate([c_ctx[None, :], c, jnp.zeros((16 - 1 - nb, D_MODEL), F32)], axis=0)
        mod = _modulation(cvec, w_mod[l], b_mod[l]).reshape(16, N_MOD, D_MODEL)
        y_prompt, (k_nat, v_nat, c_fin, m_fin) = _trunk(y_prompt, mod[0:1], False, w, None, tm=256)
        bp, tp = x_prompt.shape[0], x_prompt.shape[1]
        ks.append(k_nat.reshape(bp, tp, KV_HEADS, HEAD_DIM))
        vs.append(v_nat.reshape(bp, tp, KV_HEADS, HEAD_DIM))
        c_fin = c_fin.reshape(bp, 2, M_HEADS, M_DK, 2 * M_DV)
        cs.append(c_fin[..., 0:M_DV])
        ns.append(c_fin[..., M_DV])
        ms.append(m_fin)
        ctx = (cache_k[:, l], cache_v[:, l], state_C[:, l], state_n[:, l], state_m[:, l])
        y_sample, _ = _trunk(y_sample, mod[1:1 + nb], True, w, ctx, tm=512)
    dt = x_prompt.dtype
    return (y_prompt, y_sample, jnp.stack(ks, axis=1), jnp.stack(vs, axis=1),
            jnp.stack(cs, axis=1).astype(dt), jnp.stack(ns, axis=1).astype(dt),
            jnp.stack(ms, axis=1).astype(dt))
```

```python
import functools

import jax
import jax.numpy as jnp
from jax import lax
from jax.experimental import pallas as pl
from jax.experimental.pallas import tpu as pltpu

F32 = jnp.float32
BF16 = jnp.bfloat16

D_MODEL = 1024
D_FF = 2816
M_HEADS = 8
M_DK = 64
M_DV = 128
A_HEADS = 16
KV_HEADS = 4
GROUP = A_HEADS // KV_HEADS
HEAD_DIM = 64
GRID_W = 64
ROPE_THETA = 10000.0
N_MOD = 9
EPS = 1e-6
LOG2E = 1.4426950408889634
LN2 = 0.6931471805599453
MIX_SIZES = (512, 512, 1024, 1024, 32, 1024, 256, 256, 2048)

MCHUNK = 128
MCHUNKS_PER_STEP = 4
TOKEN_TILE = 512
SUBTILES = 4
VMEM_LIMIT_BYTES = 56 * 1024 * 1024

_NT_DIMS = (((1,), (1,)), ((), ()))


def _dot(a, b):
    return jnp.dot(a, b, preferred_element_type=F32)


def _dot_nt(a, b):
    return lax.dot_general(a, b, _NT_DIMS, preferred_element_type=F32)


def _rms_rows(x, gain_row):
    ms = jnp.mean(x * x, axis=-1, keepdims=True)
    return x * lax.rsqrt(ms + EPS) * gain_row


def _sigmoid(x):
    return jax.nn.sigmoid(x)


def _params(*sem):
    return pltpu.CompilerParams(dimension_semantics=sem, vmem_limit_bytes=VMEM_LIMIT_BYTES)


def _resident(shape):
    nd = len(shape)
    return pl.BlockSpec(shape, lambda *_: (0,) * nd, pipeline_mode=pl.Buffered(1))


def _mod_kernel(c_ref, w_ref, b_ref, o_ref):
    c = c_ref[...]
    a = c * _sigmoid(c)
    o_ref[...] = jnp.dot(a, w_ref[...], preferred_element_type=F32,
                         precision=lax.Precision.HIGHEST) + b_ref[...]


def _modulation(cvec, w_mod, b_mod):
    rows = cvec.shape[0]
    n = w_mod.shape[1]
    bn = D_MODEL
    return pl.pallas_call(
        _mod_kernel,
        grid=(n // bn,),
        in_specs=[pl.BlockSpec((rows, D_MODEL), lambda j: (0, 0)),
                  pl.BlockSpec((D_MODEL, bn), lambda j: (0, j)),
                  pl.BlockSpec((1, bn), lambda j: (0, j))],
        out_specs=pl.BlockSpec((rows, bn), lambda j: (0, j)),
        out_shape=jax.ShapeDtypeStruct((rows, n), F32),
        compiler_params=_params("arbitrary"),
        name="modulation",
    )(cvec, w_mod, b_mod.reshape(1, n))


FFN_BOUNDS = (0, 1536, D_FF)


def _ffn_kernel(x_ref, mod_ref, npre_ref, npost_ref, wi_ref, wo_ref, o_ref, *, j0, bounds):
    sh = mod_ref[0, j0:j0 + 1, :]
    sc = mod_ref[0, j0 + 1:j0 + 2, :]
    g = mod_ref[0, j0 + 2:j0 + 3, :]
    tm = x_ref.shape[1]
    rows = tm // SUBTILES

    def pre(r):
        x = x_ref[0, r * rows:(r + 1) * rows, :]
        return (_rms_rows(x, npre_ref[...]) * (1.0 + sc) + sh).astype(BF16)

    def matmuls(h):
        acc = None
        for c0, c1 in zip(bounds[:-1], bounds[1:]):
            a = _dot(h, wi_ref[:, c0:c1])
            b = _dot(h, wi_ref[:, D_FF + c0:D_FF + c1])
            act = (a * _sigmoid(a) * b).astype(BF16)
            part = _dot(act, wo_ref[c0:c1, :])
            acc = part if acc is None else acc + part
        return acc

    def post(r, acc):
        x = x_ref[0, r * rows:(r + 1) * rows, :]
        o_ref[0, r * rows:(r + 1) * rows, :] = x + 0.5 * g * _rms_rows(acc, npost_ref[...])

    hs = [pre(r) for r in range(SUBTILES)]
    prev = None
    for r in range(SUBTILES):
        acc = matmuls(hs[r])
        if prev is not None:
            post(r - 1, prev)
        prev = acc
    post(SUBTILES - 1, prev)


def _flatten_shared(arrays, tm, per_batch_mod):
    if per_batch_mod:
        return arrays, tm
    return [a.reshape(1, -1, a.shape[-1]) for a in arrays], TOKEN_TILE


def _ffn(x, mod, npre, npost, wi, wo, which, *, j0, tm, per_batch_mod):
    shape = x.shape
    (x,), tm = _flatten_shared([x], tm, per_batch_mod)
    B, T, _ = x.shape
    mod_idx = (lambda b, i: (b, 0, 0)) if per_batch_mod else (lambda b, i: (0, 0, 0))
    weight = lambda a: pl.BlockSpec((None,) + a.shape[1:], lambda b, i: (which, 0, 0),
                                    pipeline_mode=pl.Buffered(1))
    return pl.pallas_call(
        functools.partial(_ffn_kernel, j0=j0, bounds=FFN_BOUNDS),
        grid=(B, T // tm),
        in_specs=[pl.BlockSpec((1, tm, D_MODEL), lambda b, i: (b, i, 0)),
                  pl.BlockSpec((1, N_MOD, D_MODEL), mod_idx),
                  _resident((1, D_MODEL)), _resident((1, D_MODEL)),
                  weight(wi), weight(wo)],
        out_specs=pl.BlockSpec((1, tm, D_MODEL), lambda b, i: (b, i, 0)),
        out_shape=jax.ShapeDtypeStruct(x.shape, F32),
        compiler_params=_params("parallel", "parallel"),
        name="ffn_half_step",
    )(x, mod, npre.reshape(1, D_MODEL), npost.reshape(1, D_MODEL), wi, wo).reshape(shape)


_NAT_MQ, _NAT_MV, _NAT_MO, _NAT_BG, _NAT_END = 0, 512, 1536, 2560, 4608
_TR_MK, _TR_AQ, _TR_AK, _TR_AV, _TR_MG, _TR_END = 0, 512, 1536, 1792, 2048, 2080


def _rope_swap(x):
    return jnp.concatenate([x[16:32], x[0:16], x[48:64], x[32:48]], axis=0)


def _proj_kernel(*refs, rope):
    if rope:
        (x_ref, mod_ref, npre_ref, wnat_ref, wtr_ref, gq_ref, gk_ref, cos_ref, sin_ref,
         mq_ref, mv_ref, mo_ref, bg_ref, mkT_ref, aqT_ref, kn_ref, vn_ref, mgT_ref,
         k_ref, vT_ref) = refs
    else:
        (x_ref, mod_ref, npre_ref, wnat_ref, wtr_ref, gq_ref, gk_ref,
         mq_ref, mv_ref, mo_ref, bg_ref, mkT_ref, aqT_ref, kn_ref, vn_ref, mgT_ref,
         k_ref, vT_ref) = refs
    x = x_ref[0]
    sh = mod_ref[0, 3:4, :]
    sc = mod_ref[0, 4:5, :]
    h = (_rms_rows(x, npre_ref[...]) * (1.0 + sc) + sh).astype(BF16)

    tr = _dot_nt(wtr_ref[...], h)
    mkT_ref[0] = tr[_TR_MK:_TR_AQ].astype(BF16)
    mgT_ref[0] = tr[_TR_MG:_TR_END]
    avT = tr[_TR_AV:_TR_MG]
    vn_ref[0] = avT.T

    def head_norm(xh, gain_col):
        ms = jnp.mean(xh * xh, axis=0, keepdims=True)
        xh = xh * lax.rsqrt(ms + EPS) * gain_col
        if rope:
            xh = xh * cos_ref[...] + _rope_swap(xh) * sin_ref[...]
        return xh

    def q_heads(lo, hi):
        for hh in range(lo, hi):
            r0 = _TR_AQ + hh * HEAD_DIM
            xh = head_norm(tr[r0:r0 + HEAD_DIM], gq_ref[...])
            aqT_ref[0, hh * HEAD_DIM:(hh + 1) * HEAD_DIM, :] = (
                xh * (LOG2E * HEAD_DIM ** -0.5)).astype(BF16)

    def kv_heads():
        normed = []
        for hh in range(KV_HEADS):
            r0 = hh * HEAD_DIM
            xh = head_norm(tr[_TR_AK + r0:_TR_AK + r0 + HEAD_DIM], gk_ref[...])
            normed.append(xh)
            k_ref[0, hh] = xh.T.astype(BF16)
            vT_ref[0, hh] = avT[r0:r0 + HEAD_DIM].astype(BF16)
        kn_ref[0] = jnp.concatenate(normed, axis=0).T

    mq_ref[0] = (_dot(h, wnat_ref[:, _NAT_MQ:_NAT_MV]) * (M_DK ** -0.5)).astype(BF16)
    q_heads(0, A_HEADS // 2)
    mv_ref[0] = _dot(h, wnat_ref[:, _NAT_MV:_NAT_MO]).astype(BF16)
    q_heads(A_HEADS // 2, A_HEADS)
    mo_ref[0] = _dot(h, wnat_ref[:, _NAT_MO:_NAT_BG]).astype(BF16)
    kv_heads()
    half = (_NAT_END - _NAT_BG) // 2
    bg_ref[0, :, 0:half] = _dot(h, wnat_ref[:, _NAT_BG:_NAT_BG + half]).astype(BF16)
    bg_ref[0, :, half:2 * half] = _dot(h, wnat_ref[:, _NAT_BG + half:_NAT_END]).astype(BF16)


def _projection(x, mod, npre, wnat, wtr, gq, gk, rope_tables, *, tm, per_batch_mod):
    B, T, _ = x.shape
    rope = rope_tables is not None
    mod_idx = (lambda b, i: (b, 0, 0)) if per_batch_mod else (lambda b, i: (0, 0, 0))
    tok = lambda n: pl.BlockSpec((1, tm, n), lambda b, i: (b, i, 0))
    feat = lambda n: pl.BlockSpec((1, n, tm), lambda b, i: (b, 0, i))
    in_specs = [tok(D_MODEL), pl.BlockSpec((1, N_MOD, D_MODEL), mod_idx),
                _resident((1, D_MODEL)), _resident(wnat.shape), _resident(wtr.shape),
                _resident((HEAD_DIM, 1)), _resident((HEAD_DIM, 1))]
    args = [x, mod, npre.reshape(1, D_MODEL), wnat, wtr, gq, gk]
    if rope:
        in_specs += [pl.BlockSpec((HEAD_DIM, tm), lambda b, i: (0, i))] * 2
        args += list(rope_tables)
    out_shape = [jax.ShapeDtypeStruct((B, T, 512), BF16),
                 jax.ShapeDtypeStruct((B, T, 1024), BF16),
                 jax.ShapeDtypeStruct((B, T, 1024), BF16),
                 jax.ShapeDtypeStruct((B, T, 2048), BF16),
                 jax.ShapeDtypeStruct((B, 512, T), BF16),
                 jax.ShapeDtypeStruct((B, 1024, T), BF16),
                 jax.ShapeDtypeStruct((B, T, 256), F32),
                 jax.ShapeDtypeStruct((B, T, 256), F32),
                 jax.ShapeDtypeStruct((B, 32, T), F32),
                 jax.ShapeDtypeStruct((B, KV_HEADS, T, HEAD_DIM), BF16),
                 jax.ShapeDtypeStruct((B, KV_HEADS, HEAD_DIM, T), BF16)]
    out_specs = [tok(512), tok(1024), tok(1024), tok(2048),
                 feat(512), feat(1024), tok(256), tok(256), feat(32),
                 pl.BlockSpec((1, KV_HEADS, tm, HEAD_DIM), lambda b, i: (b, 0, i, 0)),
                 pl.BlockSpec((1, KV_HEADS, HEAD_DIM, tm), lambda b, i: (b, 0, 0, i))]
    return pl.pallas_call(
        functools.partial(_proj_kernel, rope=rope),
        grid=(B, T // tm),
        in_specs=in_specs, out_specs=out_specs, out_shape=out_shape,
        compiler_params=_params("parallel", "parallel"),
        name="mixer_in_proj",
    )(*args)


def _log_sigmoid(x):
    return jnp.minimum(x, 0.0) - jnp.log1p(jnp.exp(-jnp.abs(x)))


def _gate_kernel(g_ref, bias_ref, rows_ref, stats_ref, *, L):
    T = g_ref.shape[2]
    lane = lax.broadcasted_iota(jnp.int32, (M_HEADS, T), 1) & (L - 1)
    for s in range(g_ref.shape[0]):
        g = g_ref[s] + bias_ref[...]
        for d in range(2):
            li = g[16 * d:16 * d + 8] * LOG2E
            lf = _log_sigmoid(g[16 * d + 8:16 * d + 16]) * LOG2E

            def scan(v, op, fill):
                k = 1
                while k < L:
                    if d == 0:
                        sh, ok = pltpu.roll(v, k, 1), lane >= k
                    else:
                        sh, ok = pltpu.roll(v, T - k, 1), lane < L - k
                    v = op(v, jnp.where(ok, sh, fill))
                    k *= 2
                return v

            cum = scan(lf, jnp.add, 0.0)
            a = li - cum
            cm = scan(a, jnp.maximum, -jnp.inf)
            rows_ref[s, d, 0:8, :] = a
            rows_ref[s, d, 8:16, :] = cm
            rows_ref[s, d, 16:24, :] = cum
            for c in range(T // L):
                e = c * L + (L - 1 if d == 0 else 0)
                stats_ref[s, d, c, 0:8, :] = jnp.broadcast_to(cum[:, e:e + 1], (M_HEADS, 128))
                stats_ref[s, d, c, 8:16, :] = jnp.broadcast_to(cm[:, e:e + 1], (M_HEADS, 128))


def _gate_scan(mgT, bias_col, *, L):
    B, _, T = mgT.shape
    nc = T // L
    sb = max(1, min(B, 2048 // T))
    assert B % sb == 0
    return pl.pallas_call(
        functools.partial(_gate_kernel, L=L),
        grid=(B // sb,),
        in_specs=[pl.BlockSpec((sb, 32, T), lambda b: (b, 0, 0)), _resident((32, 1))],
        out_specs=[pl.BlockSpec((sb, 2, 24, T), lambda b: (b, 0, 0, 0)),
                   pl.BlockSpec((sb, 2, nc, 16, 128), lambda b: (b, 0, 0, 0, 0))],
        out_shape=[jax.ShapeDtypeStruct((B, 2, 24, T), F32),
                   jax.ShapeDtypeStruct((B, 2, nc, 16, 128), F32)],
        compiler_params=_params("parallel"),
        name="mlstm_gate_scan",
    )(mgT, bias_col)


def _mlstm_kernel(qf_ref, qb_ref, kf_ref, kb_ref, vf_ref, vb_ref, rf_ref, rb_ref, cf_ref, cb_ref,
                  sf_ref, sb_ref, c0_ref, m0_ref, hf_ref, hb_ref, cst_ref, mst_ref, *, L, ch, group):
    j = pl.program_id(1)

    @pl.when(j == 0)
    def _():
        cst_ref[...] = c0_ref[...]
        mst_ref[...] = m0_ref[...]

    row_i = lax.broadcasted_iota(jnp.int32, (L, L), 0)
    col_i = lax.broadcasted_iota(jnp.int32, (L, L), 1)
    lane = lax.broadcasted_iota(jnp.int32, (L, 128), 1)
    ones_blk = jnp.ones((L, M_DV), BF16)
    dirs = ((qf_ref, kf_ref, vf_ref, rf_ref, cf_ref, sf_ref, hf_ref),
            (qb_ref, kb_ref, vb_ref, rb_ref, cb_ref, sb_ref, hb_ref))
    for cc in range(ch):
        sub = (cc, ch - 1 - cc)
        tok = tuple(slice(c * L, (c + 1) * L) for c in sub)
        per_dir = []
        for d, refs in enumerate(dirs):
            s_ref = refs[5]
            tot = s_ref[0, 0, sub[d], 0:8, :]
            amax = s_ref[0, 0, sub[d], 8:16, :]
            m_old = mst_ref[0, d]
            m_last = jnp.maximum(m_old, amax)
            a_prev = jnp.exp2(m_old - m_last)
            mst_ref[0, d] = tot + m_last
            mask = (col_i <= row_i) if d == 0 else (col_i >= row_i)
            per_dir.append((mask, m_old, m_last, a_prev, refs[4][0, 0, tok[d], :]))

        units = [(d, h) for d in range(2) for h in range(M_HEADS)]
        for g0 in range(0, len(units), group):
            grp = units[g0:g0 + group]
            st = []
            pair_state = {}
            for d, h in grp:
                q_ref, k_ref, v_ref = dirs[d][0:3]
                p, e = divmod(h, 2)
                q_pair = q_ref[0, tok[d], 128 * p:128 * p + 128]
                keep = (lane < M_DK) if e == 0 else (lane >= M_DK)
                qm = jnp.where(keep, q_pair, jnp.zeros_like(q_pair))
                if (d, p) not in pair_state:
                    c_pair = cst_ref[0, d, p]
                    pair_state[d, p] = (c_pair, c_pair.astype(BF16))
                c_pair, c_pair_bf = pair_state[d, p]
                v_h = v_ref[0, tok[d], M_DV * h:M_DV * h + M_DV]
                st.append(dict(
                    qm=qm, c_pair_bf=c_pair_bf,
                    s=_dot(qm, k_ref[0, 128 * p:128 * p + 128, tok[d]]),
                    c_own=c_pair[M_DK * e:M_DK * e + M_DK],
                    v_aug=jnp.concatenate([v_h, ones_blk], axis=1)))
            for u, (d, h) in zip(st, grp):
                mask, m_old, m_last, a_prev, cols = per_dir[d]
                a_row = dirs[d][3][0, 0, h:h + 1, tok[d]]
                m_h = m_old[h:h + 1, :]
                big_m = jnp.maximum(m_h, cols[:, 8 + h:9 + h])
                u["decay"] = jnp.exp2(jnp.where(mask, a_row - big_m, -jnp.inf))
                u["w"] = jnp.exp2(m_h - big_m)
                u["floor"] = jnp.exp2(-(cols[:, 16 + h:17 + h] + big_m))
                u["kscale"] = jnp.exp2(a_row - m_last[h:h + 1, :])
            for u in st:
                lhs = jnp.concatenate([(u["s"] * u["decay"]).astype(BF16),
                                       u["w"].astype(BF16) * u["qm"]], axis=1)
                u["num_aug"] = _dot(lhs, jnp.concatenate([u["v_aug"], u["c_pair_bf"]], axis=0))
            for u, (d, h) in zip(st, grp):
                num_aug = u["num_aug"]
                den = num_aug[:, M_DV:2 * M_DV]
                dirs[d][6][0, tok[d], M_DV * h:M_DV * h + M_DV] = (
                    num_aug[:, 0:M_DV] / jnp.maximum(jnp.abs(den), u["floor"]))
            for u, (d, h) in zip(st, grp):
                p, e = divmod(h, 2)
                k_h = dirs[d][1][0, M_DK * h:M_DK * h + M_DK, tok[d]]
                k_scaled = (k_h.astype(F32) * u["kscale"]).astype(BF16)
                ap = per_dir[d][3][h:h + 1, :]
                cst_ref[0, d, p, M_DK * e:M_DK * e + M_DK, :] = (
                    jnp.concatenate([ap, ap], axis=1) * u["c_own"] + _dot(k_scaled, u["v_aug"]))


def _mlstm(mq, mkT, mv, rows, cols, stats, c0aug, m0rep, *, L):
    assert L == 128, "decay matrices are built as whole (L, 128) vregs"
    B, T, _ = mq.shape
    ch = min(MCHUNKS_PER_STEP, T // L)
    nb = T // (L * ch)
    lb = L * ch
    fwd = lambda b, j: j
    bwd = lambda b, j: nb - 1 - j

    def tok(n, cidx):
        return pl.BlockSpec((1, lb, n), lambda b, j: (b, cidx(b, j), 0))

    def feat(n, cidx):
        return pl.BlockSpec((1, n, lb), lambda b, j: (b, 0, cidx(b, j)))

    def rows_spec(d, cidx):
        return pl.BlockSpec((1, 1, 24, lb), lambda b, j: (b, d, 0, cidx(b, j)))

    def cols_spec(d, cidx):
        return pl.BlockSpec((1, 1, lb, 24), lambda b, j: (b, d, cidx(b, j), 0))

    def stats_spec(d, cidx):
        return pl.BlockSpec((1, 1, ch, 16, 128), lambda b, j: (b, d, cidx(b, j), 0, 0))

    state_c = pl.BlockSpec((1, 2, 4, 128, 256), lambda b, j: (b, 0, 0, 0, 0))
    state_m = pl.BlockSpec((1, 2, 8, 128), lambda b, j: (b, 0, 0, 0))
    return pl.pallas_call(
        functools.partial(_mlstm_kernel, L=L, ch=ch, group=4),
        grid=(B, nb),
        in_specs=[tok(512, fwd), tok(512, bwd), feat(512, fwd), feat(512, bwd),
                  tok(1024, fwd), tok(1024, bwd),
                  rows_spec(0, fwd), rows_spec(1, bwd), cols_spec(0, fwd), cols_spec(1, bwd),
                  stats_spec(0, fwd), stats_spec(1, bwd), state_c, state_m],
        out_specs=[tok(1024, fwd), tok(1024, bwd), state_c, state_m],
        out_shape=[jax.ShapeDtypeStruct((B, T, 1024), F32),
                   jax.ShapeDtypeStruct((B, T, 1024), F32),
                   jax.ShapeDtypeStruct((B, 2, 4, 128, 256), F32),
                   jax.ShapeDtypeStruct((B, 2, 8, 128), F32)],
        compiler_params=_params("parallel", "arbitrary"),
        name="mlstm_chunks",
    )(mq, mq, mkT, mkT, mv, mv, rows, rows, cols, cols, stats, stats, c0aug, m0rep)


def _attn_kernel(*refs, tq, nq, kc, pc, has_ctx):
    if has_ctx:
        q_ref, k_ref, vT_ref, kc_ref, vcT_ref, o_ref, s_scr, m_scr, acc_scr, l_scr = refs
    else:
        q_ref, k_ref, vT_ref, o_ref, s_scr, m_scr, acc_scr, l_scr = refs
        kc_ref = vcT_ref = None
    n = GROUP * tq
    T = k_ref.shape[2]
    S = s_scr.shape[1]

    def keys(r0, rows):
        if r0 < T:
            return k_ref[0, 0, r0:r0 + rows, :]
        return kc_ref[0, 0, r0 - T:r0 - T + rows, :]

    def values_t(r0, rows):
        if r0 < T:
            return vT_ref[0, 0, :, r0:r0 + rows]
        return vcT_ref[0, 0, :, r0 - T:r0 - T + rows]

    def block_start(j):
        return j * tq if isinstance(j, int) else pl.multiple_of(j * tq, tq)

    def run(scores, weigh_slot, emit):
        if emit is not None:
            emit_slot, j = emit
            acc_all = acc_scr[emit_slot]
            oT = acc_all[0:HEAD_DIM] / acc_all[HEAD_DIM:HEAD_DIM + 1]
            o_ref[0, pl.ds(block_start(j), tq), :] = jnp.concatenate(
                [oT[:, g * tq:(g + 1) * tq].T for g in range(GROUP)], axis=1).astype(BF16)
        scores_slot = None
        if scores is not None:
            scores_slot, j = scores
            qT = jnp.concatenate([q_ref[0, g, :, pl.ds(block_start(j), tq)]
                                  for g in range(GROUP)], axis=1)
            pm = None
        if weigh_slot is not None:
            m = m_scr[weigh_slot][0:1, :]
            acc = None
            ones_rows = jnp.ones((16, pc), BF16)
        for c in range(S // pc):
            if scores_slot is not None:
                for r0 in range(c * pc, (c + 1) * pc, kc):
                    s_c = _dot(keys(r0, kc), qT)
                    s_scr[scores_slot, r0:r0 + kc, :] = s_c
                    part = jnp.max(s_c.reshape(kc // 8, 8, n), axis=0)
                    pm = part if pm is None else jnp.maximum(pm, part)
            if weigh_slot is not None:
                p = jnp.exp2(s_scr[weigh_slot, c * pc:(c + 1) * pc, :] - m)
                v_aug = jnp.concatenate([values_t(c * pc, pc), ones_rows], axis=0)
                part = _dot(v_aug, p.astype(BF16))
                acc = part if acc is None else acc + part
        if scores_slot is not None:
            m_scr[scores_slot] = jnp.broadcast_to(jnp.max(pm, axis=0, keepdims=True), (8, n))
        if weigh_slot is not None:
            acc_scr[weigh_slot] = acc

    if nq == 1:
        run((0, 0), None, None)
        run(None, 0, None)
        run(None, None, (0, 0))
        return
    run((0, 0), None, None)
    run((1, 1), 0, None)

    def pair(k, carry):
        run((0, 2 * k), 1, (0, 2 * k - 2))
        run((1, 2 * k + 1), 0, (1, 2 * k - 1))
        return carry

    lax.fori_loop(1, nq // 2, pair, 0)
    run(None, 1, (0, nq - 2))
    run(None, None, (1, nq - 1))


def _attention(aqT, k, vT, k_ctx, vT_ctx, *, tq):
    B, _, T = aqT.shape
    P = 0 if k_ctx is None else k_ctx.shape[2]
    S = T + P
    nq = T // tq
    assert nq == 1 or nq % 2 == 0
    n = GROUP * tq
    q4 = aqT.reshape(B, A_HEADS, HEAD_DIM, T)
    head = lambda b, kv: (b, kv, 0, 0)
    in_specs = [pl.BlockSpec((1, GROUP, HEAD_DIM, T), head),
                pl.BlockSpec((1, 1, T, HEAD_DIM), head),
                pl.BlockSpec((1, 1, HEAD_DIM, T), head)]
    args = [q4, k, vT]
    if P:
        in_specs += [pl.BlockSpec((1, 1, P, HEAD_DIM), head), pl.BlockSpec((1, 1, HEAD_DIM, P), head)]
        args += [k_ctx, vT_ctx]
    return pl.pallas_call(
        functools.partial(_attn_kernel, tq=tq, nq=nq, kc=256, pc=256, has_ctx=bool(P)),
        grid=(B, KV_HEADS),
        in_specs=in_specs,
        out_specs=pl.BlockSpec((1, T, GROUP * HEAD_DIM), lambda b, kv: (b, 0, kv)),
        out_shape=jax.ShapeDtypeStruct((B, T, A_HEADS * HEAD_DIM), BF16),
        scratch_shapes=[pltpu.VMEM((2, S, n), F32), pltpu.VMEM((2, 8, n), F32),
                        pltpu.VMEM((2, HEAD_DIM + 16, n), F32), pltpu.VMEM((2, 8, n), F32)],
        compiler_params=_params("parallel", "parallel"),
        name="block_attention",
    )(*args)


def _merge_kernel(x_ref, mod_ref, hf_ref, hb_ref, mo_ref, ha_ref, bg_ref, mn_ref, npost_ref,
                  wbm_ref, wba_ref, wout_ref, o_ref):
    hm = hf_ref[0] + hb_ref[0]
    parts = []
    for h in range(M_HEADS):
        xh = hm[:, M_DV * h:M_DV * (h + 1)]
        ms = jnp.mean(xh * xh, axis=-1, keepdims=True)
        parts.append(xh * lax.rsqrt(ms + EPS))
    hn = jnp.concatenate(parts, axis=1) * mn_ref[...]
    hmo = (hn * _sigmoid(mo_ref[0].astype(F32))).astype(BF16)
    ym = _dot(hmo, wbm_ref[...])
    ya = _dot(ha_ref[0], wba_ref[...])
    bg = bg_ref[0].astype(F32)
    y = _sigmoid(bg[:, 0:D_MODEL]) * ym + _sigmoid(bg[:, D_MODEL:2 * D_MODEL]) * ya
    yo = _dot(y.astype(BF16), wout_ref[...])
    g1 = mod_ref[0, 5:6, :]
    o_ref[0] = x_ref[0] + g1 * _rms_rows(yo, npost_ref[...])


def _merge(x, mod, hf, hb, mo, ha, bg, mnorm, npost, wbm, wba, wout, *, tm, per_batch_mod):
    shape = x.shape
    (x, hf, hb, mo, ha, bg), tm = _flatten_shared([x, hf, hb, mo, ha, bg], tm, per_batch_mod)
    B, T, _ = x.shape
    mod_idx = (lambda b, i: (b, 0, 0)) if per_batch_mod else (lambda b, i: (0, 0, 0))
    tok = lambda n: pl.BlockSpec((1, tm, n), lambda b, i: (b, i, 0))
    return pl.pallas_call(
        _merge_kernel,
        grid=(B, T // tm),
        in_specs=[tok(D_MODEL), pl.BlockSpec((1, N_MOD, D_MODEL), mod_idx),
                  tok(1024), tok(1024), tok(1024), tok(1024), tok(2048),
                  _resident((1, 1024)), _resident((1, D_MODEL)),
                  _resident(wbm.shape), _resident(wba.shape), _resident(wout.shape)],
        out_specs=tok(D_MODEL),
        out_shape=jax.ShapeDtypeStruct(x.shape, F32),
        compiler_params=_params("parallel", "parallel"),
        name="mixer_merge",
    )(x, mod, hf, hb, mo, ha, bg, mnorm.reshape(1, 1024), npost.reshape(1, D_MODEL), wbm, wba,
      wout).reshape(shape)


def _rope_tables(T):
    pairs = HEAD_DIM // 4
    t = jnp.arange(T)
    pos = jnp.stack([t // GRID_W, t % GRID_W], axis=0).astype(F32)
    inv_freq = ROPE_THETA ** (-jnp.arange(pairs, dtype=F32) / pairs)
    ang = pos[:, None, :] * inv_freq[None, :, None]
    cos, sin = jnp.cos(ang), jnp.sin(ang)
    cos_t = jnp.concatenate([cos[0], cos[0], cos[1], cos[1]], axis=0)
    sin_t = jnp.concatenate([-sin[0], sin[0], -sin[1], sin[1]], axis=0)
    return cos_t, sin_t


def _trunk(x, mod, per_batch_mod, w, ctx, *, tm):
    B, T, _ = x.shape
    L = MCHUNK
    x = _ffn(x, mod, w["npre"][0], w["npost"][0], w["wi"], w["wo"], 0,
             j0=0, tm=tm, per_batch_mod=per_batch_mod)
    rope_tables = _rope_tables(T) if ctx is not None else None
    mq, mv, mo, bg, mkT, aqT, k_nat, v_nat, mgT, k_tok, vT = _projection(
        x, mod, w["npre"][1], w["wnat"], w["wtr"], w["gq"], w["gk"], rope_tables,
        tm=tm, per_batch_mod=per_batch_mod)

    rows, stats = _gate_scan(mgT, w["gate_bias"], L=L)
    cols = jnp.swapaxes(rows, 2, 3)
    if ctx is None:
        c0aug = jnp.zeros((B, 2, 4, 128, 256), F32)
        m0rep = jnp.zeros((B, 2, M_HEADS, 128), F32)
    else:
        c0, n0, m0 = ctx[2], ctx[3], ctx[4]
        n0rep = jnp.broadcast_to(n0[..., None], c0.shape)
        c0aug = jnp.concatenate([c0, n0rep], axis=-1).reshape(B, 2, 4, 128, 256)
        m0rep = jnp.broadcast_to(m0[..., None] * LOG2E, (B, 2, M_HEADS, 128))
    hf, hb, c_fin, m_fin = _mlstm(mq, mkT, mv, rows, cols, stats, c0aug, m0rep, L=L)
    m_fin = m_fin[..., 0] * LN2

    if ctx is None:
        k_ctx = v_ctx = None
    else:
        k_ctx = jnp.transpose(ctx[0], (0, 2, 1, 3)).astype(BF16)
        v_ctx = jnp.transpose(ctx[1], (0, 2, 3, 1)).astype(BF16)
    ha = _attention(aqT, k_tok, vT, k_ctx, v_ctx, tq=128 if ctx is not None else T)

    x = _merge(x, mod, hf, hb, mo, ha, bg, w["mnorm"], w["npost"][1], w["wbm"], w["wba"], w["wout"],
               tm=tm, per_batch_mod=per_batch_mod)
    x = _ffn(x, mod, w["npre"][2], w["npost"][2], w["wi"], w["wo"], 1,
             j0=6, tm=tm, per_batch_mod=per_batch_mod)
    return x, (k_nat, v_nat, c_fin, m_fin)


def _layer_weights(norm_pre, norm_post, ffn_w_in, ffn_w_out, w_in_mix, gate_bias, mlstm_norm,
                   qk_norm, w_branch_m, w_branch_a, w_out):
    o = [0]
    for s in MIX_SIZES:
        o.append(o[-1] + s)
    w_mix = w_in_mix.astype(BF16)
    col = lambda i: w_mix[:, o[i]:o[i + 1]]
    wnat = jnp.concatenate([col(0), col(2), col(3), col(8)], axis=1)
    wtr = jnp.concatenate([col(1), col(5), col(6), col(7), col(4)], axis=1).T
    return dict(
        npre=norm_pre, npost=norm_post,
        wi=ffn_w_in.astype(BF16), wo=ffn_w_out.astype(BF16),
        wnat=wnat, wtr=wtr,
        gq=qk_norm[0].reshape(HEAD_DIM, 1), gk=qk_norm[1].reshape(HEAD_DIM, 1),
        gate_bias=gate_bias.reshape(32, 1), mnorm=mlstm_norm,
        wbm=w_branch_m.astype(BF16), wba=w_branch_a.astype(BF16), wout=w_out.astype(BF16))


def kernel(x_prompt, x_sample, c, cache_k, cache_v, state_C, state_n, state_m, c_ctx, w_mod, b_mod,
           norm_pre, norm_post, ffn_w_in, ffn_w_out, w_in_mix, mlstm_gate_bias, mlstm_norm, qk_norm,
           w_branch_m, w_branch_a, w_out):
    depth = w_mod.shape[0]
    nb = c.shape[0]
    y_prompt, y_sample = x_prompt, x_sample
    ks, vs, cs, ns, ms = [], [], [], [], []
    for l in range(depth):
        w = _layer_weights(norm_pre[l], norm_post[l], ffn_w_in[l], ffn_w_out[l], w_in_mix[l],
                           mlstm_gate_bias[l], mlstm_norm[l], qk_norm[l], w_branch_m[l],
                           w_branch_a[l], w_out[l])
        cvec = jnp.concatenate([c_ctx[None, :], c, jnp.zeros((16 - 1 - nb, D_MODEL), F32)], axis=0)
        mod = _modulation(cvec, w_mod[l], b_mod[l]).reshape(16, N_MOD, D_MODEL)
        y_prompt, (k_nat, v_nat, c_fin, m_fin) = _trunk(y_prompt, mod[0:1], False, w, None, tm=256)
        bp, tp = x_prompt.shape[0], x_prompt.shape[1]
        ks.append(k_nat.reshape(bp, tp, KV_HEADS, HEAD_DIM))
        vs.append(v_nat.reshape(bp, tp, KV_HEADS, HEAD_DIM))
        c_fin = c_fin.reshape(bp, 2, M_HEADS, M_DK, 2 * M_DV)
        cs.append(c_fin[..., 0:M_DV])
        ns.append(c_fin[..., M_DV])
        ms.append(m_fin)
        ctx = (cache_k[:, l], cache_v[:, l], state_C[:, l], state_n[:, l], state_m[:, l])
        y_sample, _ = _trunk(y_sample, mod[1:1 + nb], True, w, ctx, tm=512)
    dt = x_prompt.dtype
    return (y_prompt, y_sample, jnp.stack(ks, axis=1), jnp.stack(vs, axis=1),
            jnp.stack(cs, axis=1).astype(dt), jnp.stack(ns, axis=1).astype(dt),
            jnp.stack(ms, axis=1).astype(dt))
```

```python
import functools

import jax
import jax.numpy as jnp
from jax import lax
from jax.experimental import pallas as pl
from jax.experimental.pallas import tpu as pltpu

F32 = jnp.float32
BF16 = jnp.bfloat16

D_MODEL = 1024
D_FF = 2816
M_HEADS = 8
M_DK = 64
M_DV = 128
A_HEADS = 16
KV_HEADS = 4
GROUP = A_HEADS // KV_HEADS
HEAD_DIM = 64
GRID_W = 64
ROPE_THETA = 10000.0
N_MOD = 9
EPS = 1e-6
LOG2E = 1.4426950408889634
LN2 = 0.6931471805599453
MIX_SIZES = (512, 512, 1024, 1024, 32, 1024, 256, 256, 2048)

MCHUNK = 128
MCHUNKS_PER_STEP = 4
TOKEN_TILE = 512
SUBTILES = 4
VMEM_LIMIT_BYTES = 56 * 1024 * 1024

_NT_DIMS = (((1,), (1,)), ((), ()))


def _dot(a, b):
    return jnp.dot(a, b, preferred_element_type=F32)


def _dot_nt(a, b):
    return lax.dot_general(a, b, _NT_DIMS, preferred_element_type=F32)


def _rms_rows(x, gain_row):
    ms = jnp.mean(x * x, axis=-1, keepdims=True)
    return x * lax.rsqrt(ms + EPS) * gain_row


def _sigmoid(x):
    return jax.nn.sigmoid(x)


def _params(*sem):
    return pltpu.CompilerParams(dimension_semantics=sem, vmem_limit_bytes=VMEM_LIMIT_BYTES)


def _resident(shape):
    nd = len(shape)
    return pl.BlockSpec(shape, lambda *_: (0,) * nd, pipeline_mode=pl.Buffered(1))


def _mod_kernel(c_ref, w_ref, b_ref, o_ref):
    c = c_ref[...]
    a = c * _sigmoid(c)
    o_ref[...] = jnp.dot(a, w_ref[...], preferred_element_type=F32,
                         precision=lax.Precision.HIGHEST) + b_ref[...]


def _modulation(cvec, w_mod, b_mod):
    rows = cvec.shape[0]
    n = w_mod.shape[1]
    bn = D_MODEL
    return pl.pallas_call(
        _mod_kernel,
        grid=(n // bn,),
        in_specs=[pl.BlockSpec((rows, D_MODEL), lambda j: (0, 0)),
                  pl.BlockSpec((D_MODEL, bn), lambda j: (0, j)),
                  pl.BlockSpec((1, bn), lambda j: (0, j))],
        out_specs=pl.BlockSpec((rows, bn), lambda j: (0, j)),
        out_shape=jax.ShapeDtypeStruct((rows, n), F32),
        compiler_params=_params("arbitrary"),
        name="modulation",
    )(cvec, w_mod, b_mod.reshape(1, n))


FFN_BOUNDS = (0, 1536, D_FF)


def _ffn_kernel(x_ref, mod_ref, npre_ref, npost_ref, wi_ref, wo_ref, o_ref, *, j0, bounds):
    sh = mod_ref[0, j0:j0 + 1, :]
    sc = mod_ref[0, j0 + 1:j0 + 2, :]
    g = mod_ref[0, j0 + 2:j0 + 3, :]
    tm = x_ref.shape[1]
    rows = tm // SUBTILES

    def pre(r):
        x = x_ref[0, r * rows:(r + 1) * rows, :]
        return (_rms_rows(x, npre_ref[...]) * (1.0 + sc) + sh).astype(BF16)

    def matmuls(h):
        acc = None
        for c0, c1 in zip(bounds[:-1], bounds[1:]):
            a = _dot(h, wi_ref[:, c0:c1])
            b = _dot(h, wi_ref[:, D_FF + c0:D_FF + c1])
            act = (a * _sigmoid(a) * b).astype(BF16)
            part = _dot(act, wo_ref[c0:c1, :])
            acc = part if acc is None else acc + part
        return acc

    def post(r, acc):
        x = x_ref[0, r * rows:(r + 1) * rows, :]
        o_ref[0, r * rows:(r + 1) * rows, :] = x + 0.5 * g * _rms_rows(acc, npost_ref[...])

    hs = [pre(r) for r in range(SUBTILES)]
    prev = None
    for r in range(SUBTILES):
        acc = matmuls(hs[r])
        if prev is not None:
            post(r - 1, prev)
        prev = acc
    post(SUBTILES - 1, prev)


def _flatten_shared(arrays, tm, per_batch_mod):
    if per_batch_mod:
        return arrays, tm
    return [a.reshape(1, -1, a.shape[-1]) for a in arrays], TOKEN_TILE


def _ffn(x, mod, npre, npost, wi, wo, which, *, j0, tm, per_batch_mod):
    shape = x.shape
    (x,), tm = _flatten_shared([x], tm, per_batch_mod)
    B, T, _ = x.shape
    mod_idx = (lambda b, i: (b, 0, 0)) if per_batch_mod else (lambda b, i: (0, 0, 0))
    weight = lambda a: pl.BlockSpec((None,) + a.shape[1:], lambda b, i: (which, 0, 0),
                                    pipeline_mode=pl.Buffered(1))
    return pl.pallas_call(
        functools.partial(_ffn_kernel, j0=j0, bounds=FFN_BOUNDS),
        grid=(B, T // tm),
        in_specs=[pl.BlockSpec((1, tm, D_MODEL), lambda b, i: (b, i, 0)),
                  pl.BlockSpec((1, N_MOD, D_MODEL), mod_idx),
                  _resident((1, D_MODEL)), _resident((1, D_MODEL)),
                  weight(wi), weight(wo)],
        out_specs=pl.BlockSpec((1, tm, D_MODEL), lambda b, i: (b, i, 0)),
        out_shape=jax.ShapeDtypeStruct(x.shape, F32),
        compiler_params=_params("parallel", "parallel"),
        name="ffn_half_step",
    )(x, mod, npre.reshape(1, D_MODEL), npost.reshape(1, D_MODEL), wi, wo).reshape(shape)


_NAT_MQ, _NAT_MV, _NAT_MO, _NAT_BG, _NAT_END = 0, 512, 1536, 2560, 4608
_TR_MK, _TR_AQ, _TR_AK, _TR_AV, _TR_MG, _TR_END = 0, 512, 1536, 1792, 2048, 2080


def _rope_swap(x):
    return jnp.concatenate([x[16:32], x[0:16], x[48:64], x[32:48]], axis=0)


def _proj_kernel(*refs, rope):
    if rope:
        (x_ref, mod_ref, npre_ref, wnat_ref, wtr_ref, gq_ref, gk_ref, cos_ref, sin_ref,
         mq_ref, mv_ref, mo_ref, bg_ref, mkT_ref, aqT_ref, kn_ref, vn_ref, mgT_ref,
         k_ref, vT_ref) = refs
    else:
        (x_ref, mod_ref, npre_ref, wnat_ref, wtr_ref, gq_ref, gk_ref,
         mq_ref, mv_ref, mo_ref, bg_ref, mkT_ref, aqT_ref, kn_ref, vn_ref, mgT_ref,
         k_ref, vT_ref) = refs
    x = x_ref[0]
    sh = mod_ref[0, 3:4, :]
    sc = mod_ref[0, 4:5, :]
    h = (_rms_rows(x, npre_ref[...]) * (1.0 + sc) + sh).astype(BF16)

    tr = _dot_nt(wtr_ref[...], h)
    mkT_ref[0] = tr[_TR_MK:_TR_AQ].astype(BF16)
    mgT_ref[0] = tr[_TR_MG:_TR_END]
    avT = tr[_TR_AV:_TR_MG]
    vn_ref[0] = avT.T

    def head_norm(xh, gain_col):
        ms = jnp.mean(xh * xh, axis=0, keepdims=True)
        xh = xh * lax.rsqrt(ms + EPS) * gain_col
        if rope:
            xh = xh * cos_ref[...] + _rope_swap(xh) * sin_ref[...]
        return xh

    def q_heads(lo, hi):
        for hh in range(lo, hi):
            r0 = _TR_AQ + hh * HEAD_DIM
            xh = head_norm(tr[r0:r0 + HEAD_DIM], gq_ref[...])
            aqT_ref[0, hh * HEAD_DIM:(hh + 1) * HEAD_DIM, :] = (
                xh * (LOG2E * HEAD_DIM ** -0.5)).astype(BF16)

    def kv_heads():
        normed = []
        for hh in range(KV_HEADS):
            r0 = hh * HEAD_DIM
            xh = head_norm(tr[_TR_AK + r0:_TR_AK + r0 + HEAD_DIM], gk_ref[...])
            normed.append(xh)
            k_ref[0, hh] = xh.T.astype(BF16)
            vT_ref[0, hh] = avT[r0:r0 + HEAD_DIM].astype(BF16)
        kn_ref[0] = jnp.concatenate(normed, axis=0).T

    mq_ref[0] = (_dot(h, wnat_ref[:, _NAT_MQ:_NAT_MV]) * (M_DK ** -0.5)).astype(BF16)
    q_heads(0, A_HEADS // 2)
    mv_ref[0] = _dot(h, wnat_ref[:, _NAT_MV:_NAT_MO]).astype(BF16)
    q_heads(A_HEADS // 2, A_HEADS)
    mo_ref[0] = _dot(h, wnat_ref[:, _NAT_MO:_NAT_BG]).astype(BF16)
    kv_heads()
    half = (_NAT_END - _NAT_BG) // 2
    bg_ref[0, :, 0:half] = _dot(h, wnat_ref[:, _NAT_BG:_NAT_BG + half]).astype(BF16)
    bg_ref[0, :, half:2 * half] = _dot(h, wnat_ref[:, _NAT_BG + half:_NAT_END]).astype(BF16)


def _projection(x, mod, npre, wnat, wtr, gq, gk, rope_tables, *, tm, per_batch_mod):
    B, T, _ = x.shape
    rope = rope_tables is not None
    mod_idx = (lambda b, i: (b, 0, 0)) if per_batch_mod else (lambda b, i: (0, 0, 0))
    tok = lambda n: pl.BlockSpec((1, tm, n), lambda b, i: (b, i, 0))
    feat = lambda n: pl.BlockSpec((1, n, tm), lambda b, i: (b, 0, i))
    in_specs = [tok(D_MODEL), pl.BlockSpec((1, N_MOD, D_MODEL), mod_idx),
                _resident((1, D_MODEL)), _resident(wnat.shape), _resident(wtr.shape),
                _resident((HEAD_DIM, 1)), _resident((HEAD_DIM, 1))]
    args = [x, mod, npre.reshape(1, D_MODEL), wnat, wtr, gq, gk]
    if rope:
        in_specs += [pl.BlockSpec((HEAD_DIM, tm), lambda b, i: (0, i))] * 2
        args += list(rope_tables)
    out_shape = [jax.ShapeDtypeStruct((B, T, 512), BF16),
                 jax.ShapeDtypeStruct((B, T, 1024), BF16),
                 jax.ShapeDtypeStruct((B, T, 1024), BF16),
                 jax.ShapeDtypeStruct((B, T, 2048), BF16),
                 jax.ShapeDtypeStruct((B, 512, T), BF16),
                 jax.ShapeDtypeStruct((B, 1024, T), BF16),
                 jax.ShapeDtypeStruct((B, T, 256), F32),
                 jax.ShapeDtypeStruct((B, T, 256), F32),
                 jax.ShapeDtypeStruct((B, 32, T), F32),
                 jax.ShapeDtypeStruct((B, KV_HEADS, T, HEAD_DIM), BF16),
                 jax.ShapeDtypeStruct((B, KV_HEADS, HEAD_DIM, T), BF16)]
    out_specs = [tok(512), tok(1024), tok(1024), tok(2048),
                 feat(512), feat(1024), tok(256), tok(256), feat(32),
                 pl.BlockSpec((1, KV_HEADS, tm, HEAD_DIM), lambda b, i: (b, 0, i, 0)),
                 pl.BlockSpec((1, KV_HEADS, HEAD_DIM, tm), lambda b, i: (b, 0, 0, i))]
    return pl.pallas_call(
        functools.partial(_proj_kernel, rope=rope),
        grid=(B, T // tm),
        in_specs=in_specs, out_specs=out_specs, out_shape=out_shape,
        compiler_params=_params("parallel", "parallel"),
        name="mixer_in_proj",
    )(*args)


def _log_sigmoid(x):
    return jnp.minimum(x, 0.0) - jnp.log1p(jnp.exp(-jnp.abs(x)))


def _gate_kernel(g_ref, bias_ref, rows_ref, stats_ref, *, L):
    T = g_ref.shape[2]
    lane = lax.broadcasted_iota(jnp.int32, (M_HEADS, T), 1) & (L - 1)
    for s in range(g_ref.shape[0]):
        g = g_ref[s] + bias_ref[...]
        for d in range(2):
            li = g[16 * d:16 * d + 8] * LOG2E
            lf = _log_sigmoid(g[16 * d + 8:16 * d + 16]) * LOG2E

            def scan(v, op, fill):
                k = 1
                while k < L:
                    if d == 0:
                        sh, ok = pltpu.roll(v, k, 1), lane >= k
                    else:
                        sh, ok = pltpu.roll(v, T - k, 1), lane < L - k
                    v = op(v, jnp.where(ok, sh, fill))
                    k *= 2
                return v

            cum = scan(lf, jnp.add, 0.0)
            a = li - cum
            cm = scan(a, jnp.maximum, -jnp.inf)
            rows_ref[s, d, 0:8, :] = a
            rows_ref[s, d, 8:16, :] = cm
            rows_ref[s, d, 16:24, :] = cum
            for c in range(T // L):
                e = c * L + (L - 1 if d == 0 else 0)
                stats_ref[s, d, c, 0:8, :] = jnp.broadcast_to(cum[:, e:e + 1], (M_HEADS, 128))
                stats_ref[s, d, c, 8:16, :] = jnp.broadcast_to(cm[:, e:e + 1], (M_HEADS, 128))


def _gate_scan(mgT, bias_col, *, L):
    B, _, T = mgT.shape
    nc = T // L
    sb = max(1, min(B, 2048 // T))
    assert B % sb == 0
    return pl.pallas_call(
        functools.partial(_gate_kernel, L=L),
        grid=(B // sb,),
        in_specs=[pl.BlockSpec((sb, 32, T), lambda b: (b, 0, 0)), _resident((32, 1))],
        out_specs=[pl.BlockSpec((sb, 2, 24, T), lambda b: (b, 0, 0, 0)),
                   pl.BlockSpec((sb, 2, nc, 16, 128), lambda b: (b, 0, 0, 0, 0))],
        out_shape=[jax.ShapeDtypeStruct((B, 2, 24, T), F32),
                   jax.ShapeDtypeStruct((B, 2, nc, 16, 128), F32)],
        compiler_params=_params("parallel"),
        name="mlstm_gate_scan",
    )(mgT, bias_col)


def _mlstm_kernel(qf_ref, qb_ref, kf_ref, kb_ref, vf_ref, vb_ref, rf_ref, rb_ref, cf_ref, cb_ref,
                  sf_ref, sb_ref, c0_ref, m0_ref, hf_ref, hb_ref, cfin_ref, nfin_ref, mst_ref,
                  cst_ref, *, L, ch, group):
    j = pl.program_id(1)

    @pl.when(j == 0)
    def _():
        cst_ref[...] = c0_ref[0]
        mst_ref[...] = m0_ref[...]

    row_i = lax.broadcasted_iota(jnp.int32, (L, L), 0)
    col_i = lax.broadcasted_iota(jnp.int32, (L, L), 1)
    lane = lax.broadcasted_iota(jnp.int32, (L, 128), 1)
    ones_blk = jnp.ones((L, M_DV), BF16)
    dirs = ((qf_ref, kf_ref, vf_ref, rf_ref, cf_ref, sf_ref, hf_ref),
            (qb_ref, kb_ref, vb_ref, rb_ref, cb_ref, sb_ref, hb_ref))
    for cc in range(ch):
        sub = (cc, ch - 1 - cc)
        tok = tuple(slice(c * L, (c + 1) * L) for c in sub)
        per_dir = []
        for d, refs in enumerate(dirs):
            s_ref = refs[5]
            tot = s_ref[0, 0, sub[d], 0:8, :]
            amax = s_ref[0, 0, sub[d], 8:16, :]
            m_old = mst_ref[0, d]
            m_last = jnp.maximum(m_old, amax)
            a_prev = jnp.exp2(m_old - m_last)
            mst_ref[0, d] = tot + m_last
            mask = (col_i <= row_i) if d == 0 else (col_i >= row_i)
            per_dir.append((mask, m_old, m_last, a_prev, refs[4][0, 0, tok[d], :]))

        units = [(d, h) for d in range(2) for h in range(M_HEADS)]
        for g0 in range(0, len(units), group):
            grp = units[g0:g0 + group]
            st = []
            pair_state = {}
            for d, h in grp:
                q_ref, k_ref, v_ref = dirs[d][0:3]
                p, e = divmod(h, 2)
                q_pair = q_ref[0, tok[d], 128 * p:128 * p + 128]
                keep = (lane < M_DK) if e == 0 else (lane >= M_DK)
                qm = jnp.where(keep, q_pair, jnp.zeros_like(q_pair))
                if (d, p) not in pair_state:
                    c_pair = cst_ref[d, p]
                    pair_state[d, p] = (c_pair, c_pair.astype(BF16))
                c_pair, c_pair_bf = pair_state[d, p]
                v_h = v_ref[0, tok[d], M_DV * h:M_DV * h + M_DV]
                st.append(dict(
                    qm=qm, c_pair_bf=c_pair_bf,
                    s=_dot(qm, k_ref[0, 128 * p:128 * p + 128, tok[d]]),
                    c_own=c_pair[M_DK * e:M_DK * e + M_DK],
                    v_aug=jnp.concatenate([v_h, ones_blk], axis=1)))
            for u, (d, h) in zip(st, grp):
                mask, m_old, m_last, a_prev, cols = per_dir[d]
                a_row = dirs[d][3][0, 0, h:h + 1, tok[d]]
                m_h = m_old[h:h + 1, :]
                big_m = jnp.maximum(m_h, cols[:, 8 + h:9 + h])
                u["decay"] = jnp.exp2(jnp.where(mask, a_row - big_m, -jnp.inf))
                u["w"] = jnp.exp2(m_h - big_m)
                u["floor"] = jnp.exp2(-(cols[:, 16 + h:17 + h] + big_m))
                u["kscale"] = jnp.exp2(a_row - m_last[h:h + 1, :])
            for u in st:
                lhs = jnp.concatenate([(u["s"] * u["decay"]).astype(BF16),
                                       u["w"].astype(BF16) * u["qm"]], axis=1)
                u["num_aug"] = _dot(lhs, jnp.concatenate([u["v_aug"], u["c_pair_bf"]], axis=0))
            for u, (d, h) in zip(st, grp):
                num_aug = u["num_aug"]
                den = num_aug[:, M_DV:2 * M_DV]
                dirs[d][6][0, tok[d], M_DV * h:M_DV * h + M_DV] = (
                    num_aug[:, 0:M_DV] / jnp.maximum(jnp.abs(den), u["floor"]))
            for u, (d, h) in zip(st, grp):
                p, e = divmod(h, 2)
                k_h = dirs[d][1][0, M_DK * h:M_DK * h + M_DK, tok[d]]
                k_scaled = (k_h.astype(F32) * u["kscale"]).astype(BF16)
                ap = per_dir[d][3][h:h + 1, :]
                cst_ref[d, p, M_DK * e:M_DK * e + M_DK, :] = (
                    jnp.concatenate([ap, ap], axis=1) * u["c_own"] + _dot(k_scaled, u["v_aug"]))

    @pl.when(j == pl.num_programs(1) - 1)
    def _():
        cfin_ref[0] = cst_ref[:, :, :, 0:M_DV]
        nfin_ref[0] = cst_ref[:, :, :, M_DV:2 * M_DV]


def _mlstm(mq, mkT, mv, rows, cols, stats, c0aug, m0rep, *, L):
    assert L == 128, "decay matrices are built as whole (L, 128) vregs"
    B, T, _ = mq.shape
    ch = min(MCHUNKS_PER_STEP, T // L)
    nb = T // (L * ch)
    lb = L * ch
    fwd = lambda b, j: j
    bwd = lambda b, j: nb - 1 - j

    def tok(n, cidx):
        return pl.BlockSpec((1, lb, n), lambda b, j: (b, cidx(b, j), 0))

    def feat(n, cidx):
        return pl.BlockSpec((1, n, lb), lambda b, j: (b, 0, cidx(b, j)))

    def rows_spec(d, cidx):
        return pl.BlockSpec((1, 1, 24, lb), lambda b, j: (b, d, 0, cidx(b, j)))

    def cols_spec(d, cidx):
        return pl.BlockSpec((1, 1, lb, 24), lambda b, j: (b, d, cidx(b, j), 0))

    def stats_spec(d, cidx):
        return pl.BlockSpec((1, 1, ch, 16, 128), lambda b, j: (b, d, cidx(b, j), 0, 0))

    state_c = pl.BlockSpec((1, 2, 4, 128, 256), lambda b, j: (b, 0, 0, 0, 0))
    state_half = pl.BlockSpec((1, 2, 4, 128, M_DV), lambda b, j: (b, 0, 0, 0, 0))
    state_m = pl.BlockSpec((1, 2, 8, 128), lambda b, j: (b, 0, 0, 0))
    return pl.pallas_call(
        functools.partial(_mlstm_kernel, L=L, ch=ch, group=4),
        grid=(B, nb),
        in_specs=[tok(512, fwd), tok(512, bwd), feat(512, fwd), feat(512, bwd),
                  tok(1024, fwd), tok(1024, bwd),
                  rows_spec(0, fwd), rows_spec(1, bwd), cols_spec(0, fwd), cols_spec(1, bwd),
                  stats_spec(0, fwd), stats_spec(1, bwd), state_c, state_m],
        out_specs=[tok(1024, fwd), tok(1024, bwd), state_half, state_half, state_m],
        out_shape=[jax.ShapeDtypeStruct((B, T, 1024), F32),
                   jax.ShapeDtypeStruct((B, T, 1024), F32),
                   jax.ShapeDtypeStruct((B, 2, 4, 128, M_DV), F32),
                   jax.ShapeDtypeStruct((B, 2, 4, 128, M_DV), F32),
                   jax.ShapeDtypeStruct((B, 2, 8, 128), F32)],
        scratch_shapes=[pltpu.VMEM((2, 4, 128, 2 * M_DV), F32)],
        compiler_params=_params("parallel", "arbitrary"),
        name="mlstm_chunks",
    )(mq, mq, mkT, mkT, mv, mv, rows, rows, cols, cols, stats, stats, c0aug, m0rep)


def _attn_kernel(*refs, tq, nq, hps, kc, pc, has_ctx):
    if has_ctx:
        q_ref, k_ref, vT_ref, kc_ref, vcT_ref, o_ref, s_scr, m_scr, acc_scr = refs
    else:
        q_ref, k_ref, vT_ref, o_ref, s_scr, m_scr, acc_scr = refs
        kc_ref = vcT_ref = None
    n = GROUP * tq
    width = GROUP * HEAD_DIM
    T = k_ref.shape[2]
    S = s_scr.shape[1]

    def keys(kv, r0, rows):
        if r0 < T:
            return k_ref[0, kv, r0:r0 + rows, :]
        return kc_ref[0, kv, r0 - T:r0 - T + rows, :]

    def values_t(kv, r0, rows):
        if r0 < T:
            return vT_ref[0, kv, :, r0:r0 + rows]
        return vcT_ref[0, kv, :, r0 - T:r0 - T + rows]

    def block_start(j):
        return j * tq if isinstance(j, int) else pl.multiple_of(j * tq, tq)

    def run(scores, weigh, emit):
        if emit is not None:
            emit_slot, kv_e, j = emit
            acc_all = acc_scr[emit_slot]
            oT = acc_all[0:HEAD_DIM] / acc_all[HEAD_DIM:HEAD_DIM + 1]
            o_ref[0, pl.ds(block_start(j), tq), kv_e * width:(kv_e + 1) * width] = jnp.concatenate(
                [oT[:, g * tq:(g + 1) * tq].T for g in range(GROUP)], axis=1).astype(BF16)
        if scores is not None:
            scores_slot, kv_s, j = scores
            qT = jnp.concatenate([q_ref[0, kv_s * GROUP + g, :, pl.ds(block_start(j), tq)]
                                  for g in range(GROUP)], axis=1)
            pm = None
        if weigh is not None:
            weigh_slot, kv_w, _ = weigh
            m = m_scr[weigh_slot][0:1, :]
            acc = None
            ones_rows = jnp.ones((16, pc), BF16)
        for c in range(S // pc):
            if scores is not None:
                for r0 in range(c * pc, (c + 1) * pc, kc):
                    s_c = _dot(keys(kv_s, r0, kc), qT)
                    s_scr[scores_slot, r0:r0 + kc, :] = s_c
                    part = jnp.max(s_c.reshape(kc // 8, 8, n), axis=0)
                    pm = part if pm is None else jnp.maximum(pm, part)
            if weigh is not None:
                p = jnp.exp2(s_scr[weigh_slot, c * pc:(c + 1) * pc, :] - m)
                v_aug = jnp.concatenate([values_t(kv_w, c * pc, pc), ones_rows], axis=0)
                part = _dot(v_aug, p.astype(BF16))
                acc = part if acc is None else acc + part
        if scores is not None:
            m_scr[scores_slot] = jnp.broadcast_to(jnp.max(pm, axis=0, keepdims=True), (8, n))
        if weigh is not None:
            acc_scr[weigh_slot] = acc

    blocks = [(kv, qi) for kv in range(hps) for qi in range(nq)]
    if len(blocks) <= 8:
        for step in range(len(blocks) + 2):
            def stage(lag):
                j = step - lag
                return ((j % 2,) + blocks[j]) if 0 <= j < len(blocks) else None
            run(stage(0), stage(1), stage(2))
        return
    assert hps == 1 and nq % 2 == 0
    run((0, 0, 0), None, None)
    run((1, 0, 1), (0, 0, 0), None)

    def pair(k, carry):
        run((0, 0, 2 * k), (1, 0, 0), (0, 0, 2 * k - 2))
        run((1, 0, 2 * k + 1), (0, 0, 0), (1, 0, 2 * k - 1))
        return carry

    lax.fori_loop(1, nq // 2, pair, 0)
    run(None, (1, 0, 0), (0, 0, nq - 2))
    run(None, None, (1, 0, nq - 1))


def _attention(aqT, k, vT, k_ctx, vT_ctx, *, tq, hps):
    B, _, T = aqT.shape
    P = 0 if k_ctx is None else k_ctx.shape[2]
    S = T + P
    nq = T // tq
    n = GROUP * tq
    q4 = aqT.reshape(B, A_HEADS, HEAD_DIM, T)
    head = lambda b, kv: (b, kv, 0, 0)
    in_specs = [pl.BlockSpec((1, hps * GROUP, HEAD_DIM, T), head),
                pl.BlockSpec((1, hps, T, HEAD_DIM), head),
                pl.BlockSpec((1, hps, HEAD_DIM, T), head)]
    args = [q4, k, vT]
    if P:
        in_specs += [pl.BlockSpec((1, hps, P, HEAD_DIM), head),
                     pl.BlockSpec((1, hps, HEAD_DIM, P), head)]
        args += [k_ctx, vT_ctx]
    return pl.pallas_call(
        functools.partial(_attn_kernel, tq=tq, nq=nq, hps=hps, kc=256, pc=256, has_ctx=bool(P)),
        grid=(B, KV_HEADS // hps),
        in_specs=in_specs,
        out_specs=pl.BlockSpec((1, T, hps * GROUP * HEAD_DIM), lambda b, kv: (b, 0, kv)),
        out_shape=jax.ShapeDtypeStruct((B, T, A_HEADS * HEAD_DIM), BF16),
        scratch_shapes=[pltpu.VMEM((2, S, n), F32), pltpu.VMEM((2, 8, n), F32),
                        pltpu.VMEM((2, HEAD_DIM + 16, n), F32)],
        compiler_params=_params("parallel", "parallel"),
        name="block_attention",
    )(*args)


def _merge_kernel(x_ref, mod_ref, hf_ref, hb_ref, mo_ref, ha_ref, bg_ref, mn_ref, npost_ref,
                  wbm_ref, wba_ref, wout_ref, o_ref):
    hm = hf_ref[0] + hb_ref[0]
    parts = []
    for h in range(M_HEADS):
        xh = hm[:, M_DV * h:M_DV * (h + 1)]
        ms = jnp.mean(xh * xh, axis=-1, keepdims=True)
        parts.append(xh * lax.rsqrt(ms + EPS))
    hn = jnp.concatenate(parts, axis=1) * mn_ref[...]
    hmo = (hn * _sigmoid(mo_ref[0].astype(F32))).astype(BF16)
    ym = _dot(hmo, wbm_ref[...])
    ya = _dot(ha_ref[0], wba_ref[...])
    bg = bg_ref[0].astype(F32)
    y = _sigmoid(bg[:, 0:D_MODEL]) * ym + _sigmoid(bg[:, D_MODEL:2 * D_MODEL]) * ya
    yo = _dot(y.astype(BF16), wout_ref[...])
    g1 = mod_ref[0, 5:6, :]
    o_ref[0] = x_ref[0] + g1 * _rms_rows(yo, npost_ref[...])


def _merge(x, mod, hf, hb, mo, ha, bg, mnorm, npost, wbm, wba, wout, *, tm, per_batch_mod):
    shape = x.shape
    (x, hf, hb, mo, ha, bg), tm = _flatten_shared([x, hf, hb, mo, ha, bg], tm, per_batch_mod)
    B, T, _ = x.shape
    mod_idx = (lambda b, i: (b, 0, 0)) if per_batch_mod else (lambda b, i: (0, 0, 0))
    tok = lambda n: pl.BlockSpec((1, tm, n), lambda b, i: (b, i, 0))
    return pl.pallas_call(
        _merge_kernel,
        grid=(B, T // tm),
        in_specs=[tok(D_MODEL), pl.BlockSpec((1, N_MOD, D_MODEL), mod_idx),
                  tok(1024), tok(1024), tok(1024), tok(1024), tok(2048),
                  _resident((1, 1024)), _resident((1, D_MODEL)),
                  _resident(wbm.shape), _resident(wba.shape), _resident(wout.shape)],
        out_specs=tok(D_MODEL),
        out_shape=jax.ShapeDtypeStruct(x.shape, F32),
        compiler_params=_params("parallel", "parallel"),
        name="mixer_merge",
    )(x, mod, hf, hb, mo, ha, bg, mnorm.reshape(1, 1024), npost.reshape(1, D_MODEL), wbm, wba,
      wout).reshape(shape)


def _rope_tables(T):
    pairs = HEAD_DIM // 4
    t = jnp.arange(T)
    pos = jnp.stack([t // GRID_W, t % GRID_W], axis=0).astype(F32)
    inv_freq = ROPE_THETA ** (-jnp.arange(pairs, dtype=F32) / pairs)
    ang = pos[:, None, :] * inv_freq[None, :, None]
    cos, sin = jnp.cos(ang), jnp.sin(ang)
    cos_t = jnp.concatenate([cos[0], cos[0], cos[1], cos[1]], axis=0)
    sin_t = jnp.concatenate([-sin[0], sin[0], -sin[1], sin[1]], axis=0)
    return cos_t, sin_t


def _trunk(x, mod, per_batch_mod, w, ctx, *, tm):
    B, T, _ = x.shape
    L = MCHUNK
    x = _ffn(x, mod, w["npre"][0], w["npost"][0], w["wi"], w["wo"], 0,
             j0=0, tm=tm, per_batch_mod=per_batch_mod)
    rope_tables = _rope_tables(T) if ctx is not None else None
    mq, mv, mo, bg, mkT, aqT, k_nat, v_nat, mgT, k_tok, vT = _projection(
        x, mod, w["npre"][1], w["wnat"], w["wtr"], w["gq"], w["gk"], rope_tables,
        tm=tm, per_batch_mod=per_batch_mod)

    rows, stats = _gate_scan(mgT, w["gate_bias"], L=L)
    cols = jnp.swapaxes(rows, 2, 3)
    if ctx is None:
        c0aug = jnp.zeros((B, 2, 4, 128, 256), F32)
        m0rep = jnp.zeros((B, 2, M_HEADS, 128), F32)
    else:
        c0, n0, m0 = ctx[2], ctx[3], ctx[4]
        n0rep = jnp.broadcast_to(n0[..., None], c0.shape)
        c0aug = jnp.concatenate([c0, n0rep], axis=-1).reshape(B, 2, 4, 128, 256)
        m0rep = jnp.broadcast_to(m0[..., None] * LOG2E, (B, 2, M_HEADS, 128))
    hf, hb, c_fin, n_fin, m_fin = _mlstm(mq, mkT, mv, rows, cols, stats, c0aug, m0rep, L=L)
    m_fin = m_fin[..., 0] * LN2

    if ctx is None:
        k_ctx = v_ctx = None
    else:
        k_ctx = jnp.transpose(ctx[0], (0, 2, 1, 3)).astype(BF16)
        v_ctx = jnp.transpose(ctx[1], (0, 2, 3, 1)).astype(BF16)
    if ctx is None:
        ha = _attention(aqT, k_tok, vT, None, None, tq=T, hps=KV_HEADS)
    else:
        ha = _attention(aqT, k_tok, vT, k_ctx, v_ctx, tq=128, hps=1)

    x = _merge(x, mod, hf, hb, mo, ha, bg, w["mnorm"], w["npost"][1], w["wbm"], w["wba"], w["wout"],
               tm=tm, per_batch_mod=per_batch_mod)
    x = _ffn(x, mod, w["npre"][2], w["npost"][2], w["wi"], w["wo"], 1,
             j0=6, tm=tm, per_batch_mod=per_batch_mod)
    return x, (k_nat, v_nat, c_fin, n_fin, m_fin)


def _layer_weights(norm_pre, norm_post, ffn_w_in, ffn_w_out, w_in_mix, gate_bias, mlstm_norm,
                   qk_norm, w_branch_m, w_branch_a, w_out):
    o = [0]
    for s in MIX_SIZES:
        o.append(o[-1] + s)
    w_mix = w_in_mix.astype(BF16)
    col = lambda i: w_mix[:, o[i]:o[i + 1]]
    wnat = jnp.concatenate([col(0), col(2), col(3), col(8)], axis=1)
    wtr = jnp.concatenate([col(1), col(5), col(6), col(7), col(4)], axis=1).T
    return dict(
        npre=norm_pre, npost=norm_post,
        wi=ffn_w_in.astype(BF16), wo=ffn_w_out.astype(BF16),
        wnat=wnat, wtr=wtr,
        gq=qk_norm[0].reshape(HEAD_DIM, 1), gk=qk_norm[1].reshape(HEAD_DIM, 1),
        gate_bias=gate_bias.reshape(32, 1), mnorm=mlstm_norm,
        wbm=w_branch_m.astype(BF16), wba=w_branch_a.astype(BF16), wout=w_out.astype(BF16))


def kernel(x_prompt, x_sample, c, cache_k, cache_v, state_C, state_n, state_m, c_ctx, w_mod, b_mod,
           norm_pre, norm_post, ffn_w_in, ffn_w_out, w_in_mix, mlstm_gate_bias, mlstm_norm, qk_norm,
           w_branch_m, w_branch_a, w_out):
    depth = w_mod.shape[0]
    nb = c.shape[0]
    y_prompt, y_sample = x_prompt, x_sample
    ks, vs, cs, ns, ms = [], [], [], [], []
    for l in range(depth):
        w = _layer_weights(norm_pre[l], norm_post[l], ffn_w_in[l], ffn_w_out[l], w_in_mix[l],
                           mlstm_gate_bias[l], mlstm_norm[l], qk_norm[l], w_branch_m[l],
                           w_branch_a[l], w_out[l])
        cvec = jnp.concatenate([c_ctx[None, :], c, jnp.zeros((16 - 1 - nb, D_MODEL), F32)], axis=0)
        mod = _modulation(cvec, w_mod[l], b_mod[l]).reshape(16, N_MOD, D_MODEL)
        y_prompt, (k_nat, v_nat, c_fin, n_fin, m_fin) = _trunk(
            y_prompt, mod[0:1], False, w, None, tm=256)
        bp, tp = x_prompt.shape[0], x_prompt.shape[1]
        ks.append(k_nat.reshape(bp, tp, KV_HEADS, HEAD_DIM))
        vs.append(v_nat.reshape(bp, tp, KV_HEADS, HEAD_DIM))
        cs.append(c_fin.reshape(bp, 2, M_HEADS, M_DK, M_DV))
        ns.append(n_fin.reshape(bp, 2, M_HEADS, M_DK, M_DV)[..., 0])
        ms.append(m_fin)
        ctx = (cache_k[:, l], cache_v[:, l], state_C[:, l], state_n[:, l], state_m[:, l])
        y_sample, _ = _trunk(y_sample, mod[1:1 + nb], True, w, ctx, tm=512)
    dt = x_prompt.dtype
    return (y_prompt, y_sample, jnp.stack(ks, axis=1), jnp.stack(vs, axis=1),
            jnp.stack(cs, axis=1).astype(dt), jnp.stack(ns, axis=1).astype(dt),
            jnp.stack(ms, axis=1).astype(dt))
```
